```python
import functools
import jax
import jax.numpy as jnp
from jax import lax
import numpy as np

D_MODEL = 1024
BATCH = 8
SEQ = 2048
DEPTH = 4
DEC_BATCH = 32
DEC_SEQ = 1
PAST_LEN = 8192
PAGE_SIZE = 128

HEAD_DIM = 64
GLA_HEADS = 4
HG_HEADS = 4
DSA_HEADS = 8
DSA_KV_HEADS = 4
GLA_WIDTH = GLA_HEADS * HEAD_DIM
HG_WIDTH = HG_HEADS * HEAD_DIM
DSA_WIDTH = DSA_HEADS * HEAD_DIM
KV_WIDTH = DSA_KV_HEADS * HEAD_DIM
MIX_WIDTH = GLA_WIDTH + HG_WIDTH + DSA_WIDTH
GLA_GATE_RANK = 16
GLA_GATE_TAU = 16.0
IDX_HEADS = 8
IDX_DIM = 64
TOPK_MAX = 256
ROPE_DIM = HEAD_DIM // 4
ROPE_THETA = 500000.0
N_EXPERTS = 32
TOP_K = 4
D_FF = D_MODEL
SWIGLU_ALPHA = 1.702
SWIGLU_LIMIT = 7.0
MOE_BLOCK = 128
CHUNK = 64
Q_BLOCK = 128
PLE_DIM = 256
LN_EPS = 1e-5
DEEPNORM_ALPHA = (2 * DEPTH) ** 0.25
DEEPNORM_BETA = (8 * DEPTH) ** -0.25
IN_SIZES = (GLA_WIDTH, GLA_WIDTH, GLA_WIDTH, GLA_WIDTH, GLA_GATE_RANK,
            HG_WIDTH, HG_WIDTH, HG_WIDTH, HG_WIDTH,
            DSA_WIDTH, KV_WIDTH, KV_WIDTH, IDX_HEADS * IDX_DIM, IDX_HEADS, IDX_DIM)
IN_WIDTH = sum(IN_SIZES)

kernel_name = 'hybrid_gla_hgrn2_dsa_moe_step'


def _split_points():
    pts, acc = [], 0
    for s in IN_SIZES[:-1]:
        acc += s
        pts.append(acc)
    return pts


def layer_norm(x, g, b):
    xf = x.astype(jnp.float32)
    mu = jnp.mean(xf, -1, keepdims=True)
    var = jnp.mean(jnp.square(xf - mu), -1, keepdims=True)
    return ((xf - mu) * lax.rsqrt(var + LN_EPS) * g + b).astype(x.dtype)


def head_rms_norm(x, g):
    xf = x.astype(jnp.float32)
    return (xf * lax.rsqrt(jnp.mean(xf * xf, -1, keepdims=True) + LN_EPS) * g).astype(x.dtype)


def partial_rope(x, pos):
    half = ROPE_DIM // 2
    inv = ROPE_THETA ** (-jnp.arange(half, dtype=jnp.float32) / half)
    ang = pos.astype(jnp.float32)[:, None] * inv[None, :]
    cos = jnp.cos(ang)[None, :, None, :]
    sin = jnp.sin(ang)[None, :, None, :]
    xr = x[..., :ROPE_DIM].astype(jnp.float32)
    x1, x2 = xr[..., :half], xr[..., half:]
    rot = jnp.concatenate([x1 * cos - x2 * sin, x2 * cos + x1 * sin], -1).astype(x.dtype)
    return jnp.concatenate([rot, x[..., ROPE_DIM:]], -1)


def chunk_gated_linear_attn(q, k, v, log_g, s0, chunk):
    B, T, H, _ = q.shape
    dv = v.shape[-1]
    n = T // chunk

    def to_chunks(a):
        return a.astype(jnp.float32).reshape(B, n, chunk, H, a.shape[-1]).transpose(1, 0, 3, 2, 4)

    qc, kc, vc, gc = to_chunks(q), to_chunks(k), to_chunks(v), to_chunks(log_g)
    causal = jnp.tril(jnp.ones((chunk, chunk), dtype=bool))

    def step(S, inp):
        qb, kb, vb, gb = inp
        b = jnp.cumsum(gb, axis=2)
        o_inter = jnp.einsum('bhtd,bhde->bhte', qb * jnp.exp(b), S)
        diff = b[:, :, :, None, :] - b[:, :, None, :, :]
        decay = jnp.exp(jnp.where(causal[None, None, :, :, None], diff, -jnp.inf))
        scores = jnp.einsum('bhtd,bhsd,bhtsd->bhts', qb, kb, decay)
        o = o_inter + jnp.einsum('bhts,bhse->bhte', scores, vb)
        b_last = b[:, :, -1:, :]
        S_new = jnp.exp(b_last[:, :, 0, :])[..., None] * S + jnp.einsum(
            'bhsd,bhse->bhde', kb * jnp.exp(b_last - b), vb)
        return S_new, o

    S, o = lax.scan(step, s0.astype(jnp.float32), (qc, kc, vc, gc))
    o = o.transpose(1, 0, 3, 2, 4).reshape(B, T, H, dv)
    return o.astype(v.dtype), S


def indexer_scores(q_idx, w_idx, k_idx):
    logits = jnp.einsum('bthd,bsd->bths', q_idx.astype(jnp.float32),
                        k_idx.astype(jnp.float32)) * IDX_DIM ** -0.5
    w = w_idx.astype(jnp.float32) * IDX_HEADS ** -0.5
    return jnp.einsum('bths,bth->bts', jax.nn.relu(logits), w)


def sparse_attend(q, k_sel, v_sel, valid):
    B, Tq, H, d = q.shape
    qg = q.reshape(B, Tq, DSA_KV_HEADS, H // DSA_KV_HEADS, d).astype(jnp.float32)
    s = jnp.einsum('btkgd,btskd->btkgs', qg, k_sel.astype(jnp.float32)) * d ** -0.5
    s = jnp.where(valid[:, :, None, None, :], s, -jnp.inf)
    p = jax.nn.softmax(s, axis=-1)
    o = jnp.einsum('btkgs,btskd->btkgd', p, v_sel.astype(jnp.float32))
    return o.reshape(B, Tq, H * d).astype(q.dtype)


def _dsa_heads(q, k, v, q_idx, k_idx, pos):
    B, T = q.shape[:2]
    qh = partial_rope(q.reshape(B, T, DSA_HEADS, HEAD_DIM), pos)
    kh = partial_rope(k.reshape(B, T, DSA_KV_HEADS, HEAD_DIM), pos)
    vh = v.reshape(B, T, DSA_KV_HEADS, HEAD_DIM)
    qi = partial_rope(q_idx.reshape(B, T, IDX_HEADS, IDX_DIM), pos)
    ki = partial_rope(k_idx.reshape(B, T, 1, IDX_DIM), pos)[:, :, 0]
    return qh, kh, vh, qi, ki


def dsa_prompt(q, k, v, q_idx, w_idx, k_idx, pos):
    B, T = q.shape[:2]
    qh, kh, vh, qi, ki = _dsa_heads(q, k, v, q_idx, k_idx, pos)
    topk = min(TOPK_MAX, T // 4)
    nb = T // Q_BLOCK

    def blocks(a):
        return a.reshape(B, nb, Q_BLOCK, *a.shape[2:]).swapaxes(0, 1)

    bidx = jnp.arange(B)[:, None, None]

    def one_block(args):
        qb, qib, wb, pb = args
        scores = indexer_scores(qib, wb, ki)
        scores = jnp.where(pos[None, None, :] <= pb[None, :, None], scores, -jnp.inf)
        vals, idx = lax.top_k(scores, topk)
        valid = vals > -jnp.inf
        return sparse_attend(qb, kh[bidx, idx], vh[bidx, idx], valid)

    o = lax.map(one_block, (blocks(qh), blocks(qi), blocks(w_idx), pos.reshape(nb, Q_BLOCK)))
    o = o.swapaxes(0, 1).reshape(B, T, DSA_WIDTH)
    return o, (kh, vh, ki)


def dsa_sample(q, k, v, q_idx, w_idx, k_idx, pos, li, cache_k, cache_v, cache_kidx, page_table):
    B, T = q.shape[:2]
    qh, kh, vh, qi, ki = _dsa_heads(q, k, v, q_idx, k_idx, pos)
    past = page_table.shape[1] * PAGE_SIZE
    L = past + T
    topk = min(TOPK_MAX, L // 4)
    ki_past = cache_kidx[li, page_table].reshape(B, past, IDX_DIM)
    ki_all = jnp.concatenate([ki_past.astype(ki.dtype), ki], axis=1)
    scores = indexer_scores(qi, w_idx, ki_all)
    key_pos = jnp.arange(L, dtype=jnp.int32)
    scores = jnp.where(key_pos[None, None, :] <= pos[None, :, None], scores, -jnp.inf)
    vals, idx = lax.top_k(scores, topk)
    valid = vals > -jnp.inf
    bidx = jnp.arange(B)[:, None, None]
    pidx = jnp.minimum(idx, past - 1)
    phys = page_table[bidx, pidx // PAGE_SIZE]
    off = pidx % PAGE_SIZE
    nidx = jnp.clip(idx - past, 0, T - 1)
    in_past = (idx < past)[..., None, None]
    k_sel = jnp.where(in_past, cache_k[li, phys, off].astype(kh.dtype), kh[bidx, nidx])
    v_sel = jnp.where(in_past, cache_v[li, phys, off].astype(vh.dtype), vh[bidx, nidx])
    o = sparse_attend(qh, k_sel, v_sel, valid)
    return o, (kh, vh, ki)


def token_mixers(h, lw, lower_bound, gla_s0, hg_s0, dsa_fn):
    B, T, _ = h.shape
    (gq, gk, gv, gr, glr, hq, hf, hi, hgt, dq, dkk, dvv, iq, iw, ik) = jnp.split(
        h @ lw['w_in'], _split_points(), axis=-1)
    chunk = CHUNK if T % CHUNK == 0 else T

    def heads(a, n):
        return a.reshape(B, T, n, -1)

    log_a = jax.nn.log_sigmoid((glr @ lw['w_gla_gate'] + lw['b_gla_gate']).astype(jnp.float32)) / GLA_GATE_TAU
    o_gla, gla_s = chunk_gated_linear_attn(heads(gq, GLA_HEADS) * HEAD_DIM ** -0.5, heads(gk, GLA_HEADS),
                                           heads(gv, GLA_HEADS), heads(log_a, GLA_HEADS), gla_s0, chunk)
    o_gla = head_rms_norm(o_gla, lw['gla_norm_g']) * jax.nn.silu(heads(gr, GLA_HEADS))

    zf = hf.astype(jnp.float32)
    log_f = jnp.logaddexp(jnp.log(lower_bound), jnp.log1p(-lower_bound) + jax.nn.log_sigmoid(zf))
    k_in = (1.0 - lower_bound) * jax.nn.sigmoid(-zf)
    o_hg, hg_s = chunk_gated_linear_attn(heads(jax.nn.silu(hq), HG_HEADS), heads(k_in, HG_HEADS),
                                         heads(hi, HG_HEADS), heads(log_f, HG_HEADS), hg_s0, chunk)
    o_hg = head_rms_norm(o_hg, lw['hg_norm_g']) * jax.nn.sigmoid(heads(hgt, HG_HEADS))

    o_dsa, dsa_rows = dsa_fn(dq, dkk, dvv, iq, iw, ik)

    mixed = jnp.concatenate([o_gla.reshape(B, T, GLA_WIDTH), o_hg.reshape(B, T, HG_WIDTH), o_dsa], -1)
    return mixed @ lw['w_out'], gla_s, hg_s, dsa_rows


def moe_ffn(x, lw):
    N, D = x.shape
    logits = x.astype(jnp.float32) @ lw['w_router'].astype(jnp.float32) + lw['b_router'].astype(jnp.float32)
    top_vals, top_idx = lax.top_k(logits, TOP_K)
    gates = jax.nn.softmax(top_vals, axis=-1)
    A = N * TOP_K
    flat_e = top_idx.reshape(A).astype(jnp.int32)
    order = jnp.argsort(flat_e)
    sorted_e = flat_e[order]
    counts = jnp.zeros((N_EXPERTS,), jnp.int32).at[flat_e].add(1)
    padded = (counts + MOE_BLOCK - 1) // MOE_BLOCK * MOE_BLOCK
    pad_end = jnp.cumsum(padded)
    pad_start = pad_end - padded
    start = jnp.cumsum(counts) - counts
    dest = pad_start[sorted_e] + jnp.arange(A, dtype=jnp.int32) - start[sorted_e]
    n_blocks = (A + MOE_BLOCK - 1) // MOE_BLOCK + N_EXPERTS
    rows = jnp.full((n_blocks * MOE_BLOCK,), N, jnp.int32).at[dest].set((order // TOP_K).astype(jnp.int32))
    block_start = jnp.arange(n_blocks, dtype=jnp.int32) * MOE_BLOCK
    block_expert = jnp.minimum(jnp.searchsorted(pad_end, block_start, side='right'), N_EXPERTS - 1)
    x_pad = jnp.concatenate([x, jnp.zeros((1, D), x.dtype)], 0)

    def expert_block(args):
        r, e = args
        xb = x_pad[r]
        g = jnp.minimum(xb @ lw['w_exp_gate'][e] + lw['b_exp_gate'][e], SWIGLU_LIMIT)
        u = jnp.clip(xb @ lw['w_exp_up'][e] + lw['b_exp_up'][e], -SWIGLU_LIMIT, SWIGLU_LIMIT)
        hdn = (u + 1.0) * g * jax.nn.sigmoid(SWIGLU_ALPHA * g)
        return hdn @ lw['w_exp_down'][e] + lw['b_exp_down'][e]

    out = lax.map(expert_block, (rows.reshape(n_blocks, MOE_BLOCK), block_expert)).reshape(-1, D)
    dest_orig = jnp.zeros((A,), jnp.int32).at[order].set(dest).reshape(N, TOP_K)
    return jnp.einsum('nk,nkd->nd', gates.astype(out.dtype), out[dest_orig])


def trunk_layer(x, p_i, lw, lower_bound, gla_s0, hg_s0, dsa_fn):
    B, T, D = x.shape
    mix, gla_s, hg_s, dsa_rows = token_mixers(x, lw, lower_bound, gla_s0, hg_s0, dsa_fn)
    x = layer_norm(DEEPNORM_ALPHA * x + mix, lw['ln1_g'], lw['ln1_b'])
    ffn = moe_ffn(x.reshape(B * T, D), lw).reshape(B, T, D)
    ple = jax.nn.sigmoid(x @ lw['w_ple_gate']) * (p_i.astype(x.dtype) @ lw['w_ple_proj'])
    x = layer_norm(DEEPNORM_ALPHA * x + ffn + ple, lw['ln2_g'], lw['ln2_b'])
    return x, gla_s, hg_s, dsa_rows


def setup_inputs(seed: int = 0) -> dict:
    key = jax.random.key(seed)
    ks = iter(jax.random.split(key, 40))

    def nrm(shape, scale=1.0):
        return jax.random.normal(next(ks), shape, jnp.float32) * scale

    n_pages = PAST_LEN // PAGE_SIZE
    n_pool = (DEC_BATCH * n_pages * 5) // 4
    inp = {}
    inp['x_prompt'] = nrm((BATCH, SEQ, D_MODEL))
    inp['x_sample'] = nrm((DEC_BATCH, DEC_SEQ, D_MODEL))
    inp['cache_k'] = nrm((DEPTH, n_pool, PAGE_SIZE, DSA_KV_HEADS, HEAD_DIM))
    inp['cache_v'] = nrm((DEPTH, n_pool, PAGE_SIZE, DSA_KV_HEADS, HEAD_DIM))
    inp['cache_kidx'] = nrm((DEPTH, n_pool, PAGE_SIZE, IDX_DIM))
    inp['state_gla'] = nrm((DEPTH, DEC_BATCH, GLA_HEADS, HEAD_DIM, HEAD_DIM), 0.3)
    inp['state_hgrn'] = nrm((DEPTH, DEC_BATCH, HG_HEADS, HEAD_DIM, HEAD_DIM), 0.3)
    inp['page_table'] = jax.random.permutation(next(ks), n_pool)[: DEC_BATCH * n_pages].reshape(
        DEC_BATCH, n_pages).astype(jnp.int32)
    inp['p_prompt'] = nrm((DEPTH, BATCH, SEQ, PLE_DIM))
    inp['p_sample'] = nrm((DEPTH, DEC_BATCH, DEC_SEQ, PLE_DIM))
    inp['ln_in_g'] = 1.0 + nrm((D_MODEL,), 0.02)
    inp['ln_in_b'] = nrm((D_MODEL,), 0.02)
    inp['w_in'] = nrm((DEPTH, D_MODEL, IN_WIDTH), D_MODEL ** -0.5)
    inp['w_gla_gate'] = nrm((DEPTH, GLA_GATE_RANK, GLA_WIDTH), GLA_GATE_RANK ** -0.5)
    inp['b_gla_gate'] = nrm((DEPTH, GLA_WIDTH), 0.1)
    inp['gla_norm_g'] = 1.0 + nrm((DEPTH, HEAD_DIM), 0.02)
    inp['hg_gamma'] = nrm((DEPTH, HG_WIDTH), 0.5)
    inp['hg_norm_g'] = 1.0 + nrm((DEPTH, HEAD_DIM), 0.02)
    inp['w_out'] = nrm((DEPTH, MIX_WIDTH, D_MODEL), MIX_WIDTH ** -0.5 * DEEPNORM_BETA)
    inp['ln1_g'] = 1.0 + nrm((DEPTH, D_MODEL), 0.02)
    inp['ln1_b'] = nrm((DEPTH, D_MODEL), 0.02)
    inp['w_router'] = nrm((DEPTH, D_MODEL, N_EXPERTS), D_MODEL ** -0.5)
    inp['b_router'] = nrm((DEPTH, N_EXPERTS), 0.01)
    inp['w_exp_gate'] = nrm((DEPTH, N_EXPERTS, D_MODEL, D_FF), D_MODEL ** -0.5)
    inp['b_exp_gate'] = nrm((DEPTH, N_EXPERTS, D_FF), 0.01)
    inp['w_exp_up'] = nrm((DEPTH, N_EXPERTS, D_MODEL, D_FF), D_MODEL ** -0.5)
    inp['b_exp_up'] = nrm((DEPTH, N_EXPERTS, D_FF), 0.01)
    inp['w_exp_down'] = nrm((DEPTH, N_EXPERTS, D_FF, D_MODEL), D_FF ** -0.5 * DEEPNORM_BETA)
    inp['b_exp_down'] = nrm((DEPTH, N_EXPERTS, D_MODEL), 0.01)
    inp['ln2_g'] = 1.0 + nrm((DEPTH, D_MODEL), 0.02)
    inp['ln2_b'] = nrm((DEPTH, D_MODEL), 0.02)
    inp['w_ple_gate'] = nrm((DEPTH, D_MODEL, D_MODEL), D_MODEL ** -0.5)
    inp['w_ple_proj'] = nrm((DEPTH, PLE_DIM, D_MODEL), PLE_DIM ** -0.5 * DEEPNORM_BETA)
    return inp


def reference(x_prompt, x_sample, cache_k, cache_v, cache_kidx, state_gla, state_hgrn, page_table,
              p_prompt, p_sample, ln_in_g, ln_in_b, w_in, w_gla_gate, b_gla_gate, gla_norm_g,
              hg_gamma, hg_norm_g, w_out, ln1_g, ln1_b, w_router, b_router, w_exp_gate, b_exp_gate,
              w_exp_up, b_exp_up, w_exp_down, b_exp_down, ln2_g, ln2_b, w_ple_gate, w_ple_proj):
    B, T = x_prompt.shape[:2]
    Bd, Td = x_sample.shape[:2]
    sm = jax.nn.softmax(hg_gamma.astype(jnp.float32), axis=0)
    lower_bounds = jnp.maximum(jnp.cumsum(sm, axis=0) - sm[0:1], 0.0)
    pos_p = jnp.arange(T, dtype=jnp.int32)
    pos_s = PAST_LEN + jnp.arange(Td, dtype=jnp.int32)
    xp = layer_norm(x_prompt, ln_in_g, ln_in_b)
    xs = layer_norm(x_sample, ln_in_g, ln_in_b)
    zero_gla = jnp.zeros((B, GLA_HEADS, HEAD_DIM, HEAD_DIM), jnp.float32)
    zero_hg = jnp.zeros((B, HG_HEADS, HEAD_DIM, HEAD_DIM), jnp.float32)
    kp_l, vp_l, kip_l, sgp_l, shp_l = [], [], [], [], []
    ks_l, vs_l, kis_l, sgs_l, shs_l = [], [], [], [], []
    for li in range(DEPTH):
        lw = dict(w_in=w_in[li], w_gla_gate=w_gla_gate[li], b_gla_gate=b_gla_gate[li],
                  gla_norm_g=gla_norm_g[li], hg_norm_g=hg_norm_g[li], w_out=w_out[li],
                  ln1_g=ln1_g[li], ln1_b=ln1_b[li], w_router=w_router[li], b_router=b_router[li],
                  w_exp_gate=w_exp_gate[li], b_exp_gate=b_exp_gate[li], w_exp_up=w_exp_up[li],
                  b_exp_up=b_exp_up[li], w_exp_down=w_exp_down[li], b_exp_down=b_exp_down[li],
                  ln2_g=ln2_g[li], ln2_b=ln2_b[li], w_ple_gate=w_ple_gate[li], w_ple_proj=w_ple_proj[li])
        lb = lower_bounds[li]
        xp, sgp, shp, (kp, vp, kip) = trunk_layer(
            xp, p_prompt[li], lw, lb, zero_gla, zero_hg, functools.partial(dsa_prompt, pos=pos_p))
        xs, sgs, shs, (ksn, vsn, kisn) = trunk_layer(
            xs, p_sample[li], lw, lb, state_gla[li], state_hgrn[li],
            functools.partial(dsa_sample, pos=pos_s, li=li, cache_k=cache_k, cache_v=cache_v,
                              cache_kidx=cache_kidx, page_table=page_table))
        kp_l.append(kp); vp_l.append(vp); kip_l.append(kip); sgp_l.append(sgp); shp_l.append(shp)
        ks_l.append(ksn); vs_l.append(vsn); kis_l.append(kisn); sgs_l.append(sgs); shs_l.append(shs)
    new_k_prompt = jnp.stack(kp_l, 0)
    new_v_prompt = jnp.stack(vp_l, 0)
    new_kidx_prompt = jnp.stack(kip_l, 0)
    new_state_gla_prompt = jnp.stack(sgp_l, 0)
    new_state_hgrn_prompt = jnp.stack(shp_l, 0)
    new_k_sample = jnp.stack(ks_l, 0)
    new_v_sample = jnp.stack(vs_l, 0)
    new_kidx_sample = jnp.stack(kis_l, 0)
    new_state_gla_sample = jnp.stack(sgs_l, 0)
    new_state_hgrn_sample = jnp.stack(shs_l, 0)
    return (xp, xs, new_k_prompt, new_v_prompt, new_kidx_prompt, new_state_gla_prompt,
            new_state_hgrn_prompt, new_k_sample, new_v_sample, new_kidx_sample,
            new_state_gla_sample, new_state_hgrn_sample)
```

```python
import functools

import jax
import jax.numpy as jnp
from jax import lax
from jax.experimental import pallas as pl
from jax.experimental.pallas import tpu as pltpu

F32 = jnp.float32
BF16 = jnp.bfloat16
I32 = jnp.int32

LANES = 128
SUBLANES = 8
VMEM_LIMIT = 56 * 1024 * 1024

D_MODEL = 1024
HEAD_DIM = 64
PAGE_SIZE = 128
IDX_HEADS = 8
N_EXPERTS = 32
TOP_K = 4
TOPK_MAX = 256
ROPE_DIM = 16
ROPE_THETA = 500000.0
GLA_GATE_TAU = 16.0
SWIGLU_ALPHA = 1.702
SWIGLU_LIMIT = 7.0
LN_EPS = 1e-5
DEPTH = 4
DEEPNORM_ALPHA = (2 * DEPTH) ** 0.25

_OFF = dict(gq=0, gk=256, gv=512, gr=768, glr=1024, hq=1040, hf=1296, hi=1552, hgt=1808,
            dq=2064, dk=2576, dv=2832, iq=3088, iw=3600, ik=3608)
PROJ_W = 3712
MISC_IW = 80
MISC_GLR = 88
NEG_BIG = -1e30

GLA_CHUNK = 128
GLA_SUB = 16
GLA_EXP_CLAMP = 60.0

MOE_BLK = 256
ROW_TILE = SUBLANES * LANES


def _cparams(sem, vmem=VMEM_LIMIT):
    return pltpu.CompilerParams(dimension_semantics=sem, vmem_limit_bytes=vmem)


def _dot(a, b, precision=None):
    return jnp.dot(a, b, preferred_element_type=F32, precision=precision)


def _dot_nt(a, b, precision=None):
    return lax.dot_general(a, b, (((1,), (1,)), ((), ())), preferred_element_type=F32,
                           precision=precision)


def _sigmoid(x):
    return 1.0 / (1.0 + jnp.exp(-x))


def _log_sigmoid(x):
    return jnp.minimum(x, 0.0) - jnp.log(1.0 + jnp.exp(-jnp.abs(x)))


def _layer_norm_rows(x, g, b):
    mu = jnp.mean(x, axis=-1, keepdims=True)
    xc = x - mu
    var = jnp.mean(xc * xc, axis=-1, keepdims=True)
    return xc * lax.rsqrt(var + LN_EPS) * g + b


def _lower_bound_kernel(gam_ref, lb_ref):
    g = gam_ref[...]
    m = jnp.max(g, axis=0, keepdims=True)
    e = jnp.exp(g - m)
    sm = e / jnp.sum(e, axis=0, keepdims=True)
    acc = jnp.zeros_like(sm[0:1])
    for li in range(g.shape[0]):
        acc = acc + sm[li:li + 1]
        lb_ref[li:li + 1, :] = jnp.maximum(acc - sm[0:1], 0.0)


def _lower_bounds(hg_gamma):
    return pl.pallas_call(
        _lower_bound_kernel, out_shape=jax.ShapeDtypeStruct(hg_gamma.shape, F32),
        name="hgrn_lower_bounds")(hg_gamma.astype(F32))


def _ln_kernel(x_ref, g_ref, b_ref, o_ref):
    o_ref[...] = _layer_norm_rows(x_ref[...], g_ref[...], b_ref[...])


def _input_layer_norm(x, g, b, tm):
    n, d = x.shape
    return pl.pallas_call(
        _ln_kernel, grid=(n // tm,),
        in_specs=[pl.BlockSpec((tm, d), lambda i: (i, 0)),
                  pl.BlockSpec((1, d), lambda i: (0, 0)),
                  pl.BlockSpec((1, d), lambda i: (0, 0))],
        out_specs=pl.BlockSpec((tm, d), lambda i: (i, 0)),
        out_shape=jax.ShapeDtypeStruct((n, d), F32),
        compiler_params=_cparams(("parallel",)), name="input_ln")(x, g.reshape(1, d), b.reshape(1, d))


def _inproj_kernel(h_ref, w_ref, cos_ref, sa_ref, sb_ref,
                   g_ref, hh_ref, misc_ref, q_ref, qi_ref, k_ref, v_ref, ki_ref,
                   kb_ref, vb_ref, kid_ref):
    hb = h_ref[...].astype(BF16)
    cos, sa, sb = cos_ref[...], sa_ref[...], sb_ref[...]

    def proj(lo, width):
        return _dot(hb, w_ref[:, lo:lo + width])

    def rope(x):
        return x * cos + pltpu.roll(x, LANES - 8, 1) * sa + pltpu.roll(x, 8, 1) * sb

    g_ref[...] = proj(0, 1024)
    hh_ref[...] = proj(1024, 1024)
    for s in range(4):
        q_ref[:, s * LANES:(s + 1) * LANES] = (rope(proj(2048 + s * LANES, LANES)) * 0.125).astype(BF16)
        qi_ref[:, s * LANES:(s + 1) * LANES] = (rope(proj(3072 + s * LANES, LANES)) * 0.125).astype(BF16)
    for s in range(2):
        kr = rope(proj(2560 + s * LANES, LANES))
        k_ref[:, s * LANES:(s + 1) * LANES] = kr
        kb_ref[:, s * LANES:(s + 1) * LANES] = kr.astype(BF16)
    vv = proj(2816, 256)
    v_ref[...] = vv
    vb_ref[...] = vv.astype(BF16)
    misc = rope(proj(3584, LANES))
    misc_ref[...] = misc
    ki_ref[...] = misc[:, 0:HEAD_DIM]
    lane = lax.broadcasted_iota(I32, misc.shape, 1)
    ki_lo = jnp.where(lane < HEAD_DIM, misc, 0.0)
    kid_ref[...] = (ki_lo + pltpu.roll(ki_lo, HEAD_DIM, 1)).astype(BF16)


def _in_projection(h, w_r, cos, sa, sb, tm):
    n = h.shape[0]
    t_blocks = cos.shape[0] // tm
    row = lambda w: pl.BlockSpec((tm, w), lambda i: (i, 0))
    tab = pl.BlockSpec((tm, LANES), lambda i: (i % t_blocks, 0))
    widths = [(1024, F32), (1024, F32), (LANES, F32), (512, BF16), (512, BF16), (256, F32), (256, F32),
              (HEAD_DIM, F32), (256, BF16), (256, BF16), (LANES, BF16)]
    return pl.pallas_call(
        _inproj_kernel, grid=(n // tm,),
        in_specs=[row(D_MODEL), pl.BlockSpec((D_MODEL, PROJ_W), lambda i: (0, 0)), tab, tab, tab],
        out_specs=[row(w) for w, _ in widths],
        out_shape=[jax.ShapeDtypeStruct((n, w), dt) for w, dt in widths],
        compiler_params=_cparams(("parallel",)), name="in_projection")(h, w_r, cos, sa, sb)


def _gla_kernel(g_ref, h_ref, misc_ref, wg_ref, bg_ref, lb_ref, gn_ref, hn_ref, s0_ref,
                o_ref, sT_ref, state, *, n_chunks, t_valid):
    c_idx = pl.program_id(1)

    @pl.when(c_idx == 0)
    def _():
        state[...] = s0_ref[0]

    ch = GLA_CHUNK
    row = lax.broadcasted_iota(I32, (ch, ch), 0)
    col = lax.broadcasted_iota(I32, (ch, ch), 1)
    tri = jnp.where(col <= row, 1.0, 0.0).astype(F32)
    lane = lax.broadcasted_iota(I32, (ch, LANES), 1)
    bd = jnp.where((row < HEAD_DIM) == (col < HEAD_DIM), 1.0, 0.0).astype(F32)
    head_mean = bd * (1.0 / HEAD_DIM)
    lane32 = lax.broadcasted_iota(I32, (2 * GLA_SUB, LANES), 1)
    row32 = lax.broadcasted_iota(I32, (2 * GLA_SUB, LANES), 0)
    head_sel = jnp.where((row32 < GLA_SUB) == (lane32 < HEAD_DIM), 1.0, 0.0).astype(F32)
    lane16 = lax.broadcasted_iota(I32, (GLA_SUB, LANES), 1)

    def chunk(ci, carry):
        r0 = pl.multiple_of(ci * ch, ch)
        rows = pl.ds(r0, ch)
        gx = g_ref[rows, :]
        hx = h_ref[rows, :]
        z = _dot(misc_ref[rows, :], wg_ref[...], precision=lax.Precision.HIGHEST) + bg_ref[...]
        lg_gla = _log_sigmoid(z) * (1.0 / GLA_GATE_TAU)
        lb = lb_ref[...]
        zf = hx[:, 256:512]
        a_ = jnp.log(lb)
        c_ = jnp.log(1.0 - lb) + _log_sigmoid(zf)
        lg_hg = jnp.maximum(a_, c_) + jnp.log(1.0 + jnp.exp(-jnp.abs(a_ - c_)))
        k_hg = (1.0 - lb) * _sigmoid(-zf)
        hq = hx[:, 0:256]
        q_hg = hq * _sigmoid(hq)
        gr = gx[:, 768:1024]
        mix = ((gx[:, 0:256] * (HEAD_DIM ** -0.5), gx[:, 256:512], gx[:, 512:768], lg_gla,
                gr * _sigmoid(gr), gn_ref[...]),
               (q_hg, k_hg, hx[:, 512:768], lg_hg, _sigmoid(hx[:, 768:1024]), hn_ref[...]))
        if t_valid is not None:
            tpos = c_idx * (n_chunks * ch) + r0 + lax.broadcasted_iota(I32, (ch, 1), 0)
            live = tpos < t_valid
        for mi, (qm, km, vm, lgm, gatem, normm) in enumerate(mix):
            for p in range(2):
                sl = slice(p * LANES, (p + 1) * LANES)
                q2, k2, v2, lg2 = qm[:, sl], km[:, sl], vm[:, sl], lgm[:, sl]
                if t_valid is not None:
                    k2 = jnp.where(live, k2, 0.0)
                    lg2 = jnp.where(live, lg2, 0.0)
                b = _dot(tri, lg2, precision=lax.Precision.HIGHEST)
                b_last = b[ch - 1:ch, :]
                sT = state[2 * mi + p]
                o = _dot_nt((q2 * jnp.exp(b)).astype(BF16), sT.astype(BF16))
                v2b = v2.astype(BF16)
                parts = []
                for i in range(ch // GLA_SUB):
                    lo = i * GLA_SUB
                    n = lo + GLA_SUB
                    ref = b[lo - 1:lo, :] if i > 0 else jnp.zeros((1, LANES), F32)
                    qs = q2[lo:n, :] * jnp.exp(b[lo:n, :] - ref)
                    ks = k2[0:n, :] * jnp.exp(jnp.minimum(ref - b[0:n, :], GLA_EXP_CLAMP))
                    lhs = (jnp.concatenate([qs, qs], axis=0) * head_sel).astype(BF16)
                    a = _dot_nt(lhs, ks.astype(BF16))
                    tt = lo + (lax.broadcasted_iota(I32, (2 * GLA_SUB, n), 0) % GLA_SUB)
                    ss = lax.broadcasted_iota(I32, (2 * GLA_SUB, n), 1)
                    a = jnp.where(ss <= tt, a, 0.0)
                    oi = _dot(a.astype(BF16), v2b[0:n, :])
                    parts.append(jnp.where(lane16 < HEAD_DIM, oi[0:GLA_SUB, :], oi[GLA_SUB:, :]))
                o = o + jnp.concatenate(parts, axis=0)
                kd = k2 * jnp.exp(b_last - b)
                upd = _dot(v2.T.astype(BF16), kd.astype(BF16))
                state[2 * mi + p] = (sT * jnp.exp(b_last) + upd) * bd
                ms = _dot(o * o, head_mean, precision=lax.Precision.HIGHEST)
                on = o * lax.rsqrt(ms + LN_EPS) * normm * gatem[:, sl]
                o_ref[rows, mi * 256 + p * LANES:mi * 256 + (p + 1) * LANES] = on
        return carry

    lax.fori_loop(0, n_chunks, chunk, 0)

    @pl.when(c_idx == pl.num_programs(1) - 1)
    def _():
        sT_ref[0] = state[...]


def _gla_hgrn(g_all, h_all, misc, wg_pad, bg, lb, gn, hn, s0_bd, batch, t_pad, t_valid):
    ct = min(2 * GLA_CHUNK, t_pad)
    steps = t_pad // ct
    n = batch * t_pad
    row = lambda w: pl.BlockSpec((ct, w), lambda b, c: (b * steps + c, 0))
    const = lambda shp: pl.BlockSpec(shp, lambda b, c: tuple(0 for _ in shp))
    st = pl.BlockSpec((1, 4, LANES, LANES), lambda b, c: (b, 0, 0, 0))
    kern = functools.partial(_gla_kernel, n_chunks=ct // GLA_CHUNK,
                             t_valid=None if t_valid == t_pad else t_valid)
    return pl.pallas_call(
        kern, grid=(batch, steps),
        in_specs=[row(1024), row(1024), row(LANES), const((LANES, 256)), const((1, 256)), const((1, 256)),
                  const((1, LANES)), const((1, LANES)), st],
        out_specs=[row(512), st],
        out_shape=[jax.ShapeDtypeStruct((n, 512), F32),
                   jax.ShapeDtypeStruct((batch, 4, LANES, LANES), F32)],
        scratch_shapes=[pltpu.VMEM((4, LANES, LANES), F32)],
        compiler_params=_cparams(("parallel", "arbitrary")), name="gla_hgrn")(
            g_all, h_all, misc, wg_pad, bg, lb, gn, hn, s0_bd)


def _sort_key(x):
    bits = lax.bitcast_convert_type(x, I32)
    return jnp.where(bits < 0, bits ^ jnp.int32(0x7FFFFFFF), bits)


def _kth_largest_key(count_ge, shape, k):
    int_min = jnp.int32(-2 ** 31)

    def body(it, cand):
        trial = cand | jnp.left_shift(jnp.int32(1), 31 - it)
        cnt = count_ge(trial ^ int_min)
        return jnp.where(cnt >= k, trial, cand)

    cand = lax.fori_loop(0, 32, body, jnp.zeros(shape, I32))
    return cand ^ int_min


def _dsa_prompt_kernel(q_ref, qi_ref, misc_ref, kb_ref, vb_ref, kid_ref, o_ref,
                       sc_ref, key_ref, neg_ref, *, seq, kt, topk):
    qb = q_ref.shape[0]
    i = pl.program_id(1)
    q_lo = i * qb
    n_t = seq // kt
    n_act = (q_lo + qb - 1) // kt + 1
    row_pos = q_lo + lax.broadcasted_iota(I32, (qb, 1), 0)
    lane_q = lax.broadcasted_iota(I32, (qb, LANES), 1)
    half = (lane_q < HEAD_DIM, lane_q >= HEAD_DIM)
    w8 = misc_ref[...][:, MISC_IW:MISC_IW + IDX_HEADS] * (IDX_HEADS ** -0.5)

    sc_ref[...] = jnp.full(sc_ref.shape, -jnp.inf, F32)
    for t in range(n_t):
        @pl.when(t < n_act)
        def _(t=t):
            cols = slice(t * kt, (t + 1) * kt)
            kid = kid_ref[cols, :]
            acc = jnp.zeros((qb, kt), F32)
            for j in range(IDX_HEADS // 2):
                slab = qi_ref[:, j * LANES:(j + 1) * LANES]
                for c in range(2):
                    lhs = jnp.where(half[c], slab, jnp.zeros_like(slab))
                    logit = _dot_nt(lhs, kid)
                    acc = acc + jnp.maximum(logit, 0.0) * w8[:, 2 * j + c:2 * j + c + 1]
            col_pos = t * kt + lax.broadcasted_iota(I32, (1, kt), 1)
            sc_ref[:, cols] = jnp.where(col_pos <= row_pos, acc, -jnp.inf)
    key_ref[...] = _sort_key(sc_ref[...])

    for nt in range(1, n_t + 1):
        @pl.when(n_act == nt)
        def _(nt=nt):
            w = nt * kt

            def count_ge(thr):
                return jnp.sum(jnp.where(key_ref[:, 0:w] >= thr, 1.0, 0.0), axis=1, keepdims=True)

            thr = _kth_largest_key(count_ge, (qb, 1), float(topk))
            col_pos = lax.broadcasted_iota(I32, (1, w), 1)
            keep = (key_ref[:, 0:w] >= thr) & (col_pos <= row_pos)
            neg_ref[:, 0:w] = jnp.where(keep, 0.0, NEG_BIG)
            for p in range(2):
                kp = kb_ref[0:w, p * LANES:(p + 1) * LANES]
                vp = vb_ref[0:w, p * LANES:(p + 1) * LANES]
                for g in range(2):
                    s = 2 * p + g
                    qs = q_ref[:, s * LANES:(s + 1) * LANES]
                    outs = []
                    for c in range(2):
                        lhs = jnp.where(half[c], qs, jnp.zeros_like(qs))
                        sc = _dot_nt(lhs, kp) + neg_ref[:, 0:w]
                        m = jnp.max(sc, axis=1, keepdims=True)
                        pe = jnp.exp(sc - m)
                        l = jnp.sum(pe, axis=1, keepdims=True)
                        outs.append(_dot(pe.astype(BF16), vp) / l)
                    o_ref[:, s * LANES:(s + 1) * LANES] = jnp.where(half[0], outs[0], outs[1])


def _dsa_prompt(q, qi, misc, kb, vb, kid, batch, seq, qb=128):
    n = batch * seq
    nq = seq // qb
    kt = min(512, seq)
    topk = min(TOPK_MAX, seq // 4)
    rowq = lambda w: pl.BlockSpec((qb, w), lambda b, i: (b * nq + i, 0))
    full = lambda w: pl.BlockSpec((seq, w), lambda b, i: (b, 0))
    kern = functools.partial(_dsa_prompt_kernel, seq=seq, kt=kt, topk=topk)
    return pl.pallas_call(
        kern, grid=(batch, nq),
        in_specs=[rowq(512), rowq(512), rowq(LANES), full(256), full(256), full(LANES)],
        out_specs=rowq(512),
        out_shape=jax.ShapeDtypeStruct((n, 512), F32),
        scratch_shapes=[pltpu.VMEM((qb, seq), F32), pltpu.VMEM((qb, seq), I32), pltpu.VMEM((qb, seq), F32)],
        compiler_params=_cparams(("parallel", "arbitrary")), name="dsa_prompt")(q, qi, misc, kb, vb, kid)


PAGES_PER_STEP = 8
DSA_Q_HEADS = 8
DSA_KV_HEADS = 4


def _dsa_sample_score_kernel(pt_ref, qi_ref, w_ref, *refs):
    pages, out_ref = refs[:PAGES_PER_STEP], refs[PAGES_PER_STEP]
    qi = qi_ref[0]
    w = w_ref[0] * (IDX_HEADS ** -0.5)
    for j, pg in enumerate(pages):
        logit = _dot_nt(qi, pg[...].astype(BF16))
        out_ref[0, j:j + 1, :] = jnp.sum(jnp.maximum(logit, 0.0) * w, axis=0, keepdims=True)


def _dsa_sample_scores(page_table, cache_kidx, li, qi3, w3):
    bd, n_pages = page_table.shape
    groups = n_pages // PAGES_PER_STEP

    def page_spec(j):
        return pl.BlockSpec((None, None, PAGE_SIZE, HEAD_DIM),
                            lambda b, g, pt: (li, pt[b, g * PAGES_PER_STEP + j], 0, 0))

    grid_spec = pltpu.PrefetchScalarGridSpec(
        num_scalar_prefetch=1, grid=(bd, groups),
        in_specs=[pl.BlockSpec((1, IDX_HEADS, HEAD_DIM), lambda b, g, pt: (b, 0, 0)),
                  pl.BlockSpec((1, IDX_HEADS, 1), lambda b, g, pt: (b, 0, 0))]
        + [page_spec(j) for j in range(PAGES_PER_STEP)],
        out_specs=pl.BlockSpec((1, PAGES_PER_STEP, PAGE_SIZE), lambda b, g, pt: (b, g, 0)))
    return pl.pallas_call(
        _dsa_sample_score_kernel, grid_spec=grid_spec,
        out_shape=jax.ShapeDtypeStruct((bd, n_pages, PAGE_SIZE), F32),
        compiler_params=_cparams(("parallel", "arbitrary")), name="dsa_sample_scores")(
            page_table, qi3, w3, *([cache_kidx] * PAGES_PER_STEP))


def _dsa_sample_threshold_kernel(sc_ref, qi_ref, ki_ref, w_ref, thr_ref, self_ref, key_ref, *, topk):
    qi = qi_ref[...].astype(F32)
    ki = ki_ref[...].astype(BF16).astype(F32)
    w = w_ref[...] * (IDX_HEADS ** -0.5)
    logit = jnp.sum(qi * ki, axis=2, keepdims=True)
    self_sc = jnp.sum(jnp.maximum(logit, 0.0) * w, axis=1)
    self_key = _sort_key(self_sc)
    key_ref[...] = _sort_key(sc_ref[...])

    def count_ge(thr):
        c = jnp.sum(jnp.where(key_ref[...] >= thr, 1.0, 0.0), axis=1, keepdims=True)
        return c + jnp.where(self_key >= thr, 1.0, 0.0)

    thr = _kth_largest_key(count_ge, self_key.shape, float(topk))
    thr_ref[...] = thr
    self_ref[...] = jnp.where(self_key >= thr, 1.0, 0.0)


def _dsa_sample_threshold(scores2, qi3, ki3, w3, topk):
    bd, past = scores2.shape
    kern = functools.partial(_dsa_sample_threshold_kernel, topk=topk)
    return pl.pallas_call(
        kern,
        out_shape=[jax.ShapeDtypeStruct((bd, 1), I32), jax.ShapeDtypeStruct((bd, 1), F32)],
        scratch_shapes=[pltpu.VMEM((bd, past), I32)],
        compiler_params=pltpu.CompilerParams(vmem_limit_bytes=VMEM_LIMIT),
        name="dsa_sample_threshold")(scores2, qi3, ki3, w3)


def _dsa_sample_attn_kernel(pt_ref, qbd_ref, sc_ref, thr_ref, self_ref, kn_ref, vn_ref, *refs,
                            n_groups):
    kpages = refs[:PAGES_PER_STEP]
    vpages = refs[PAGES_PER_STEP:2 * PAGES_PER_STEP]
    o_ref, m_ref, l_ref, acc_ref = refs[2 * PAGES_PER_STEP:]
    g = pl.program_id(1)
    nh = qbd_ref.shape[1]

    @pl.when(g == 0)
    def _():
        m_ref[...] = jnp.full(m_ref.shape, NEG_BIG, F32)
        l_ref[...] = jnp.zeros(l_ref.shape, F32)
        acc_ref[...] = jnp.zeros(acc_ref.shape, F32)

    qbd = qbd_ref[0]
    thr = thr_ref[0]

    for j in range(PAGES_PER_STEP):
        keep = _sort_key(sc_ref[0, j:j + 1, :]) >= thr
        sc = jnp.where(keep, _dot_nt(qbd, kpages[j][...].astype(BF16)), NEG_BIG)
        m_old = m_ref[...]
        m_new = jnp.maximum(m_old, jnp.max(sc, axis=1, keepdims=True))
        alpha = jnp.exp(m_old - m_new)
        pe = jnp.where(keep, jnp.exp(sc - m_new), 0.0)
        l_ref[...] = alpha * l_ref[...] + jnp.sum(pe, axis=1, keepdims=True)
        acc_ref[...] = alpha * acc_ref[...] + _dot(pe.astype(BF16), vpages[j][...].astype(BF16))
        m_ref[...] = m_new

    @pl.when(g == n_groups - 1)
    def _():
        kn = kn_ref[0].astype(BF16).astype(F32)
        sc = jnp.sum(qbd.astype(F32) * kn, axis=1, keepdims=True)
        sc = jnp.where(self_ref[0] > 0.5, sc, NEG_BIG)
        m_old = m_ref[...]
        m_new = jnp.maximum(m_old, sc)
        alpha = jnp.exp(m_old - m_new)
        pe = jnp.exp(sc - m_new)
        pe = jnp.where(sc > 0.5 * NEG_BIG, pe, 0.0)
        l = alpha * l_ref[...] + pe
        acc = alpha * acc_ref[...] + pe * vn_ref[0].astype(BF16).astype(F32)
        res = acc / l
        kvh = lax.broadcasted_iota(I32, (nh, HEAD_DIM), 0) // 2
        out = jnp.zeros((nh, HEAD_DIM), F32)
        for c in range(4):
            out = out + jnp.where(kvh == c, res[:, c * HEAD_DIM:(c + 1) * HEAD_DIM], 0.0)
        o_ref[0] = out


def _dsa_sample_attention(page_table, cache_k2, cache_v2, li, qbd, scores3, thr3, self3, kn3, vn3):
    bd, n_pages = page_table.shape
    groups = n_pages // PAGES_PER_STEP
    kvw = cache_k2.shape[-1]

    def page_spec(j):
        return pl.BlockSpec((None, None, PAGE_SIZE, kvw),
                            lambda b, g, pt: (li, pt[b, g * PAGES_PER_STEP + j], 0, 0))

    per_b = lambda shp: pl.BlockSpec((1,) + shp, lambda b, g, pt: (b, 0, 0))
    grid_spec = pltpu.PrefetchScalarGridSpec(
        num_scalar_prefetch=1, grid=(bd, groups),
        in_specs=[per_b((DSA_Q_HEADS, kvw)),
                  pl.BlockSpec((1, PAGES_PER_STEP, PAGE_SIZE), lambda b, g, pt: (b, g, 0)),
                  per_b((1, 1)), per_b((1, 1)), per_b((1, kvw)), per_b((1, kvw))]
        + [page_spec(j) for j in range(PAGES_PER_STEP)] * 2,
        out_specs=per_b((DSA_Q_HEADS, HEAD_DIM)),
        scratch_shapes=[pltpu.VMEM((DSA_Q_HEADS, 1), F32), pltpu.VMEM((DSA_Q_HEADS, 1), F32),
                        pltpu.VMEM((DSA_Q_HEADS, kvw), F32)])
    kern = functools.partial(_dsa_sample_attn_kernel, n_groups=groups)
    return pl.pallas_call(
        kern, grid_spec=grid_spec,
        out_shape=jax.ShapeDtypeStruct((bd, DSA_Q_HEADS, HEAD_DIM), F32),
        compiler_params=_cparams(("parallel", "arbitrary")), name="dsa_sample_attention")(
            page_table, qbd, scores3, thr3, self3, kn3, vn3,
            *([cache_k2] * PAGES_PER_STEP), *([cache_v2] * PAGES_PER_STEP))


def _post_mix_kernel(oa_ref, od_ref, x_ref, p_ref, woa_ref, wod_ref, g1_ref, b1_ref, wr_ref, br_ref,
                     wpg_ref, wpp_ref, cnt0_ref, x1t_ref, base_ref, route_ref, cnt_ref):
    tb = x_ref.shape[0]

    @pl.when(pl.program_id(0) == 0)
    def _():
        cnt_ref[...] = cnt0_ref[...]

    mixv = _dot(oa_ref[...].astype(BF16), woa_ref[...]) + _dot(od_ref[...].astype(BF16), wod_ref[...])
    x1 = _layer_norm_rows(DEEPNORM_ALPHA * x_ref[...] + mixv, g1_ref[...], b1_ref[...])
    x1b = x1.astype(BF16)
    for j in range(SUBLANES):
        x1t_ref[pl.ds(j, tb, stride=SUBLANES), :] = x1[:, j * LANES:(j + 1) * LANES]
    ple = _sigmoid(_dot(x1b, wpg_ref[...])) * _dot(p_ref[...].astype(BF16), wpp_ref[...])
    base_ref[...] = DEEPNORM_ALPHA * x1 + ple

    logits = _dot(x1, wr_ref[...], precision=lax.Precision.HIGHEST) + br_ref[...]
    lane = lax.broadcasted_iota(I32, (tb, LANES), 1)
    lanef = lane.astype(F32)
    vals, hots = [], []
    work = logits
    for _ in range(TOP_K):
        m = jnp.max(work, axis=1, keepdims=True)
        idx = jnp.min(jnp.where(work == m, lanef, float(LANES)), axis=1, keepdims=True)
        hot = lanef == idx
        vals.append(m)
        hots.append(hot)
        work = jnp.where(hot, -jnp.inf, work)
    exps = [jnp.exp(v - vals[0]) for v in vals]
    den = exps[0] + exps[1] + exps[2] + exps[3]
    oh = jnp.zeros((tb, LANES), F32)
    for hot in hots:
        oh = oh + jnp.where(hot, 1.0, 0.0)
    r = lax.broadcasted_iota(I32, (tb, tb), 0)
    c = lax.broadcasted_iota(I32, (tb, tb), 1)
    strict = jnp.where(c < r, 1.0, 0.0).astype(BF16)
    before = _dot(strict, oh.astype(BF16)) + cnt_ref[...]
    route = jnp.zeros((tb, LANES), F32)
    for k in range(TOP_K):
        gate = exps[k] / den
        eidx = jnp.sum(jnp.where(hots[k], lanef, 0.0), axis=1, keepdims=True)
        rank = jnp.sum(jnp.where(hots[k], before, 0.0), axis=1, keepdims=True)
        route = route + jnp.where(lane == k, gate, 0.0) + jnp.where(lane == TOP_K + k, eidx, 0.0) \
            + jnp.where(lane == 2 * TOP_K + k, rank, 0.0)
    route_ref[...] = route
    cnt_ref[...] = cnt_ref[...] + jnp.sum(oh, axis=0, keepdims=True)


def _post_mix(oa, od, x, p, woa, wod, g1, b1, wr, br, wpg, wpp, cnt0, tb):
    n = x.shape[0]
    row = lambda w: pl.BlockSpec((tb, w), lambda i: (i, 0))
    const = lambda a: pl.BlockSpec(a.shape, lambda i: tuple(0 for _ in a.shape))
    return pl.pallas_call(
        _post_mix_kernel, grid=(n // tb,),
        in_specs=[row(512), row(512), row(D_MODEL), row(p.shape[1]), const(woa), const(wod), const(g1),
                  const(b1), const(wr), const(br), const(wpg), const(wpp), const(cnt0)],
        out_specs=[pl.BlockSpec((tb * SUBLANES, LANES), lambda i: (i, 0)), row(D_MODEL), row(LANES),
                   pl.BlockSpec((1, LANES), lambda i: (0, 0))],
        out_shape=[jax.ShapeDtypeStruct((n * SUBLANES, LANES), F32), jax.ShapeDtypeStruct((n, D_MODEL), F32),
                   jax.ShapeDtypeStruct((n, LANES), F32), jax.ShapeDtypeStruct((1, LANES), F32)],
        compiler_params=_cparams(("arbitrary",)), name="post_mix_router")(
            oa, od, x, p, woa, wod, g1, b1, wr, br, wpg, wpp, cnt0)


def _row_copy(src, src_row, dst, dst_row, sem):
    return pltpu.make_async_copy(src.at[pl.ds(src_row * SUBLANES, SUBLANES), :],
                                 dst.at[pl.ds(dst_row * SUBLANES, SUBLANES), :], sem)


def _moe_scatter_kernel(dest_ref, x_ref, xs_in_ref, xs_ref, sem):
    del xs_in_ref
    n_copies = dest_ref.shape[-1]

    def issue(a, carry):
        _row_copy(x_ref, a // TOP_K, xs_ref, dest_ref[0, 0, a], sem).start()
        return carry

    lax.fori_loop(0, n_copies, issue, 0)

    def drain(a, carry):
        _row_copy(x_ref, 0, xs_ref, 0, sem).wait()
        return carry

    lax.fori_loop(0, n_copies, drain, 0)


def _moe_scatter(dest3, x1t, n_slots, tb):
    n = x1t.shape[0] // SUBLANES
    zeros = jnp.zeros((n_slots * SUBLANES, LANES), F32)
    return pl.pallas_call(
        _moe_scatter_kernel, grid=(n // tb,),
        in_specs=[pl.BlockSpec((1, 1, tb * TOP_K), lambda i: (i, 0, 0), memory_space=pltpu.SMEM),
                  pl.BlockSpec((tb * SUBLANES, LANES), lambda i: (i, 0)),
                  pl.BlockSpec(memory_space=pl.ANY)],
        out_specs=pl.BlockSpec(memory_space=pl.ANY),
        out_shape=jax.ShapeDtypeStruct(zeros.shape, F32),
        scratch_shapes=[pltpu.SemaphoreType.DMA],
        input_output_aliases={2: 0},
        compiler_params=_cparams(("arbitrary",)), name="moe_scatter")(dest3, x1t, zeros)


def _moe_expert_kernel(be_ref, nu_ref, xs_ref, wg_ref, bg_ref, wu_ref, bu_ref, wd_ref, bd_ref, ys_ref,
                       wgb, wub, wdb):
    i = pl.program_id(0)
    blk = xs_ref.shape[0] // SUBLANES

    @pl.when(i < nu_ref[0])
    def _():
        prev = be_ref[jnp.maximum(i - 1, 0)]

        @pl.when((i == 0) | (prev != be_ref[i]))
        def _():
            wgb[...] = wg_ref[...].astype(BF16)
            wub[...] = wu_ref[...].astype(BF16)
            wdb[...] = wd_ref[...].astype(BF16)

        x = jnp.concatenate([xs_ref[pl.ds(j, blk, stride=SUBLANES), :] for j in range(SUBLANES)],
                            axis=1).astype(BF16)
        g = jnp.minimum(_dot(x, wgb[...]) + bg_ref[...], SWIGLU_LIMIT)
        u = jnp.clip(_dot(x, wub[...]) + bu_ref[...], -SWIGLU_LIMIT, SWIGLU_LIMIT)
        hdn = (u + 1.0) * g * _sigmoid(SWIGLU_ALPHA * g)
        y = _dot(hdn.astype(BF16), wdb[...]) + bd_ref[...]
        for j in range(SUBLANES):
            ys_ref[pl.ds(j, blk, stride=SUBLANES), :] = y[:, j * LANES:(j + 1) * LANES]

    @pl.when(i >= nu_ref[0])
    def _():
        ys_ref[...] = jnp.zeros(ys_ref.shape, F32)


def _moe_experts(block_expert, n_used, xs, wg, bg, wu, bu, wd, bd):
    n_blocks = block_expert.shape[0]
    d, f = wg.shape[1], wg.shape[2]

    def blk_idx(i, be, nu):
        return (jnp.minimum(i, nu[0] - 1), 0)

    def w_idx(i, be, nu):
        return (be[jnp.minimum(i, nu[0] - 1)], 0, 0)

    rows = pl.BlockSpec((MOE_BLK * SUBLANES, LANES), blk_idx)
    out_rows = pl.BlockSpec((MOE_BLK * SUBLANES, LANES), lambda i, be, nu: (i, 0))
    grid_spec = pltpu.PrefetchScalarGridSpec(
        num_scalar_prefetch=2, grid=(n_blocks,),
        in_specs=[rows,
                  pl.BlockSpec((None, d, f), w_idx), pl.BlockSpec((None, 1, f), w_idx),
                  pl.BlockSpec((None, d, f), w_idx), pl.BlockSpec((None, 1, f), w_idx),
                  pl.BlockSpec((None, f, d), w_idx), pl.BlockSpec((None, 1, d), w_idx)],
        out_specs=out_rows,
        scratch_shapes=[pltpu.VMEM((d, f), BF16), pltpu.VMEM((d, f), BF16), pltpu.VMEM((f, d), BF16)])
    return pl.pallas_call(
        _moe_expert_kernel, grid_spec=grid_spec,
        out_shape=jax.ShapeDtypeStruct(xs.shape, F32),
        compiler_params=_cparams(("arbitrary",)), name="moe_experts")(
            block_expert, n_used, xs, wg, bg, wu, bu, wd, bd)


def _moe_combine_kernel(dest_ref, ys_ref, route_ref, base_ref, g2_ref, b2_ref, o_ref, yg, sem):
    tb = base_ref.shape[0]
    n_copies = tb * TOP_K

    def issue(a, carry):
        n, k = a // TOP_K, a % TOP_K
        _row_copy(ys_ref, dest_ref[0, 0, a], yg, k * tb + n, sem).start()
        return carry

    lax.fori_loop(0, n_copies, issue, 0)

    def drain(a, carry):
        _row_copy(ys_ref, 0, yg, 0, sem).wait()
        return carry

    lax.fori_loop(0, n_copies, drain, 0)

    gates = route_ref[...][:, 0:TOP_K]
    cols = []
    for j in range(SUBLANES):
        acc = jnp.zeros((tb, LANES), F32)
        for k in range(TOP_K):
            acc = acc + gates[:, k:k + 1] * yg[pl.ds(k * tb * SUBLANES + j, tb, stride=SUBLANES), :]
        cols.append(acc)
    ffn = jnp.concatenate(cols, axis=1)
    o_ref[...] = _layer_norm_rows(base_ref[...] + ffn, g2_ref[...], b2_ref[...])


def _moe_combine(dest3, ys, route, base, g2, b2, tb):
    n = base.shape[0]
    row = lambda w: pl.BlockSpec((tb, w), lambda i: (i, 0))
    const = lambda a: pl.BlockSpec(a.shape, lambda i: tuple(0 for _ in a.shape))
    return pl.pallas_call(
        _moe_combine_kernel, grid=(n // tb,),
        in_specs=[pl.BlockSpec((1, 1, tb * TOP_K), lambda i: (i, 0, 0), memory_space=pltpu.SMEM),
                  pl.BlockSpec(memory_space=pl.ANY), row(LANES), row(D_MODEL), const(g2), const(b2)],
        out_specs=row(D_MODEL),
        out_shape=jax.ShapeDtypeStruct((n, D_MODEL), F32),
        scratch_shapes=[pltpu.VMEM((TOP_K * tb * SUBLANES, LANES), F32), pltpu.SemaphoreType.DMA],
        compiler_params=_cparams(("arbitrary",)), name="moe_combine")(dest3, ys, route, base, g2, b2)


def _moe_layer(x1t, route, cnt, base, lw, tb):
    n = base.shape[0]
    a = n * TOP_K
    n_blocks = a // MOE_BLK + N_EXPERTS
    counts = cnt[0, :N_EXPERTS].astype(I32)
    padded = (counts + MOE_BLK - 1) // MOE_BLK * MOE_BLK
    pad_end = jnp.cumsum(padded)
    pad_start = pad_end - padded
    top_idx = route[:, TOP_K:2 * TOP_K].astype(I32)
    rank = route[:, 2 * TOP_K:3 * TOP_K].astype(I32)
    dest = pad_start[top_idx] + rank
    dest3 = dest.reshape(n // tb, 1, tb * TOP_K)
    block_start = jnp.arange(n_blocks, dtype=I32) * MOE_BLK
    block_expert = jnp.minimum(jnp.searchsorted(pad_end, block_start, side='right'),
                               N_EXPERTS - 1).astype(I32)
    n_used = (pad_end[-1:] // MOE_BLK).astype(I32)
    xs = _moe_scatter(dest3, x1t, n_blocks * MOE_BLK, tb)
    ys = _moe_experts(block_expert, n_used, xs, lw['w_exp_gate'], lw['b_exp_gate'], lw['w_exp_up'],
                      lw['b_exp_up'], lw['w_exp_down'], lw['b_exp_down'])
    return _moe_combine(dest3, ys, route, base, lw['ln2_g'], lw['ln2_b'], tb)


_DQ_PROMPT_ORDER = (0, 2, 1, 3, 4, 6, 5, 7)


def _relayout_w_in(w, head_order):
    col = lambda name, width: w[:, _OFF[name]:_OFF[name] + width]
    dq = col('dq', 512).reshape(D_MODEL, 8, HEAD_DIM)[:, jnp.array(head_order), :].reshape(D_MODEL, 512)
    z = lambda width: jnp.zeros((D_MODEL, width), w.dtype)
    misc = jnp.concatenate([col('ik', 64), z(16), col('iw', 8), col('glr', 16), z(24)], axis=1)
    out = jnp.concatenate([col('gq', 256), col('gk', 256), col('gv', 256), col('gr', 256),
                           col('hq', 256), col('hf', 256), col('hi', 256), col('hgt', 256),
                           dq, col('dk', 256), col('dv', 256), col('iq', 512), misc], axis=1)
    return out.astype(BF16)


def _rope_tables(pos):
    half = ROPE_DIM // 2
    inv = ROPE_THETA ** (-jnp.arange(half, dtype=F32) / half)
    ang = pos.astype(F32)[:, None] * inv[None, :]
    cos, sin = jnp.cos(ang), jnp.sin(ang)
    t = pos.shape[0]
    one, zero = jnp.ones((t, HEAD_DIM - ROPE_DIM), F32), jnp.zeros((t, HEAD_DIM - ROPE_DIM), F32)
    z8 = jnp.zeros((t, half), F32)
    c64 = jnp.concatenate([cos, cos, one], axis=1)
    a64 = jnp.concatenate([-sin, z8, zero], axis=1)
    b64 = jnp.concatenate([z8, sin, zero], axis=1)
    tile2 = lambda a: jnp.concatenate([a, a], axis=1)
    return tile2(c64), tile2(a64), tile2(b64)


def _layer_weights(li, w):
    wo = w['w_out'][li]
    wod = wo[512:].reshape(8, HEAD_DIM, D_MODEL)
    wg_pad = jnp.zeros((LANES, 256), F32).at[MISC_GLR:MISC_GLR + 16].set(w['w_gla_gate'][li])
    wr = jnp.zeros((D_MODEL, LANES), F32).at[:, :N_EXPERTS].set(w['w_router'][li])
    br = jnp.full((1, LANES), NEG_BIG, F32).at[0, :N_EXPERTS].set(w['b_router'][li])
    tile2 = lambda g: jnp.concatenate([g, g]).reshape(1, LANES)
    return dict(
        w_in_p=_relayout_w_in(w['w_in'][li], _DQ_PROMPT_ORDER),
        w_in_s=_relayout_w_in(w['w_in'][li], range(8)),
        wg_pad=wg_pad, bg=w['b_gla_gate'][li].reshape(1, 256),
        gn=tile2(w['gla_norm_g'][li]), hn=tile2(w['hg_norm_g'][li]),
        woa=wo[:512].astype(BF16),
        wod_p=wod[jnp.array(_DQ_PROMPT_ORDER)].reshape(512, D_MODEL).astype(BF16),
        wod_s=wo[512:].astype(BF16),
        ln1_g=w['ln1_g'][li].reshape(1, D_MODEL), ln1_b=w['ln1_b'][li].reshape(1, D_MODEL),
        wr=wr, br=br,
        wpg=w['w_ple_gate'][li].astype(BF16), wpp=w['w_ple_proj'][li].astype(BF16),
        w_exp_gate=w['w_exp_gate'][li], b_exp_gate=w['b_exp_gate'][li].reshape(N_EXPERTS, 1, -1),
        w_exp_up=w['w_exp_up'][li], b_exp_up=w['b_exp_up'][li].reshape(N_EXPERTS, 1, -1),
        w_exp_down=w['w_exp_down'][li], b_exp_down=w['b_exp_down'][li].reshape(N_EXPERTS, 1, -1),
        ln2_g=w['ln2_g'][li].reshape(1, D_MODEL), ln2_b=w['ln2_b'][li].reshape(1, D_MODEL))


def _state_to_blockdiag(s):
    b = s.shape[0]
    st = jnp.swapaxes(s, -1, -2).reshape(b, 2, 2, HEAD_DIM, HEAD_DIM)
    z = jnp.zeros_like(st[:, :, 0])
    top = jnp.concatenate([st[:, :, 0], z], axis=-1)
    bot = jnp.concatenate([z, st[:, :, 1]], axis=-1)
    return jnp.concatenate([top, bot], axis=-2)


def _state_from_blockdiag(sbd):
    b = sbd.shape[0]
    h0 = sbd[:, :, :HEAD_DIM, :HEAD_DIM]
    h1 = sbd[:, :, HEAD_DIM:, HEAD_DIM:]
    st = jnp.stack([h0, h1], axis=2).reshape(b, 4, HEAD_DIM, HEAD_DIM)
    return jnp.swapaxes(st, -1, -2)


def kernel(x_prompt, x_sample, cache_k, cache_v, cache_kidx, state_gla, state_hgrn, page_table, p_prompt, p_sample, ln_in_g, ln_in_b, w_in, w_gla_gate, b_gla_gate, gla_norm_g, hg_gamma, hg_norm_g, w_out, ln1_g, ln1_b, w_router, b_router, w_exp_gate, b_exp_gate, w_exp_up, b_exp_up, w_exp_down, b_exp_down, ln2_g, ln2_b, w_ple_gate, w_ple_proj):
    bp, t, d = x_prompt.shape
    bs, ts, _ = x_sample.shape
    assert ts == 1 and d == D_MODEL
    depth = w_in.shape[0]
    n_pages = page_table.shape[1]
    past = n_pages * PAGE_SIZE
    n_p = bp * t
    weights = dict(w_in=w_in, w_gla_gate=w_gla_gate, b_gla_gate=b_gla_gate, gla_norm_g=gla_norm_g,
                   hg_norm_g=hg_norm_g, w_out=w_out, ln1_g=ln1_g, ln1_b=ln1_b, w_router=w_router,
                   b_router=b_router, w_exp_gate=w_exp_gate, b_exp_gate=b_exp_gate, w_exp_up=w_exp_up,
                   b_exp_up=b_exp_up, w_exp_down=w_exp_down, b_exp_down=b_exp_down, ln2_g=ln2_g,
                   ln2_b=ln2_b, w_ple_gate=w_ple_gate, w_ple_proj=w_ple_proj)

    tm_p = min(512, t)
    tb_p = min(256, t)
    lbs = _lower_bounds(hg_gamma)
    tabs_p = _rope_tables(jnp.arange(t, dtype=I32))
    tabs_s = _rope_tables(jnp.full((bs,), past, I32))
    xp = _input_layer_norm(x_prompt.reshape(n_p, d), ln_in_g, ln_in_b, tm_p)
    xs = _input_layer_norm(x_sample.reshape(bs, d), ln_in_g, ln_in_b, bs)
    cache_k2 = cache_k.reshape(cache_k.shape[0], cache_k.shape[1], PAGE_SIZE, -1)
    cache_v2 = cache_v.reshape(cache_v.shape[0], cache_v.shape[1], PAGE_SIZE, -1)
    zero_state = jnp.zeros((bp, 4, LANES, LANES), F32)
    zero_cnt = jnp.zeros((1, LANES), F32)
    topk_s = min(TOPK_MAX, (past + ts) // 4)
    head_of_row = jnp.arange(DSA_Q_HEADS)[:, None] // 2
    kv_of_lane = jnp.arange(DSA_KV_HEADS * HEAD_DIM)[None, :] // HEAD_DIM

    outs = {k: [] for k in ('kp', 'vp', 'kip', 'sgp', 'shp', 'ks', 'vs', 'kis', 'sgs', 'shs')}
    for li in range(depth):
        lw = _layer_weights(li, weights)
        lb = lbs[li].reshape(1, 256)

        (g_all, h_all, misc, q, qi, k, v, ki, kb, vb, kid) = _in_projection(xp, lw['w_in_p'], *tabs_p, tm_p)
        o_lin, st = _gla_hgrn(g_all, h_all, misc, lw['wg_pad'], lw['bg'], lb, lw['gn'], lw['hn'],
                              zero_state, bp, t, t)
        o_dsa = _dsa_prompt(q, qi, misc, kb, vb, kid, bp, t)
        x1t, base, route, cnt = _post_mix(o_lin, o_dsa, xp, p_prompt[li].reshape(n_p, -1), lw['woa'],
                                          lw['wod_p'], lw['ln1_g'], lw['ln1_b'], lw['wr'], lw['br'],
                                          lw['wpg'], lw['wpp'], zero_cnt, tb_p)
        xp = _moe_layer(x1t, route, cnt, base, lw, tb_p)
        outs['kp'].append(k.reshape(bp, t, DSA_KV_HEADS, HEAD_DIM))
        outs['vp'].append(v.reshape(bp, t, DSA_KV_HEADS, HEAD_DIM))
        outs['kip'].append(ki.reshape(bp, t, HEAD_DIM))
        outs['sgp'].append(_state_from_blockdiag(st[:, 0:2]))
        outs['shp'].append(_state_from_blockdiag(st[:, 2:4]))

        (g_all, h_all, misc, q, qi, k, v, ki, kb, vb, kid) = _in_projection(xs, lw['w_in_s'], *tabs_s, bs)
        pad = lambda a: jnp.pad(a.reshape(bs, 1, -1), ((0, 0), (0, GLA_CHUNK - 1), (0, 0))).reshape(
            bs * GLA_CHUNK, -1)
        s0 = jnp.concatenate([_state_to_blockdiag(state_gla[li]), _state_to_blockdiag(state_hgrn[li])], axis=1)
        o_lin, st = _gla_hgrn(pad(g_all), pad(h_all), pad(misc), lw['wg_pad'], lw['bg'], lb, lw['gn'],
                              lw['hn'], s0, bs, GLA_CHUNK, 1)
        o_lin = o_lin.reshape(bs, GLA_CHUNK, -1)[:, 0]
        qi3 = qi.reshape(bs, IDX_HEADS, HEAD_DIM)
        w3 = misc[:, MISC_IW:MISC_IW + IDX_HEADS].reshape(bs, IDX_HEADS, 1)
        scores = _dsa_sample_scores(page_table, cache_kidx, li, qi3, w3)
        thr, self_sel = _dsa_sample_threshold(scores.reshape(bs, past), qi3, ki.reshape(bs, 1, HEAD_DIM),
                                              w3, topk_s)
        q3 = q.reshape(bs, DSA_Q_HEADS, 1, HEAD_DIM)
        qbd = jnp.where(head_of_row == kv_of_lane,
                        jnp.tile(q3, (1, 1, DSA_KV_HEADS, 1)).reshape(bs, DSA_Q_HEADS, -1), 0).astype(BF16)
        o_dsa = _dsa_sample_attention(page_table, cache_k2, cache_v2, li, qbd, scores,
                                      thr.reshape(bs, 1, 1), self_sel.reshape(bs, 1, 1),
                                      k.reshape(bs, 1, -1), v.reshape(bs, 1, -1)).reshape(bs, 512)
        x1t, base, route, cnt = _post_mix(o_lin, o_dsa, xs, p_sample[li].reshape(bs, -1), lw['woa'],
                                          lw['wod_s'], lw['ln1_g'], lw['ln1_b'], lw['wr'], lw['br'],
                                          lw['wpg'], lw['wpp'], zero_cnt, bs)
        xs = _moe_layer(x1t, route, cnt, base, lw, bs)
        outs['ks'].append(k.reshape(bs, ts, DSA_KV_HEADS, HEAD_DIM))
        outs['vs'].append(v.reshape(bs, ts, DSA_KV_HEADS, HEAD_DIM))
        outs['kis'].append(ki.reshape(bs, ts, HEAD_DIM))
        outs['sgs'].append(_state_from_blockdiag(st[:, 0:2]))
        outs['shs'].append(_state_from_blockdiag(st[:, 2:4]))

    stack = lambda name: jnp.stack(outs[name], 0)
    return (xp.reshape(bp, t, d), xs.reshape(bs, ts, d),
            stack('kp'), stack('vp'), stack('kip'), stack('sgp'), stack('shp'),
            stack('ks'), stack('vs'), stack('kis'), stack('sgs'), stack('shs'))
```

```python
import functools

import jax
import jax.numpy as jnp
from jax import lax
from jax.experimental import pallas as pl
from jax.experimental.pallas import tpu as pltpu

F32 = jnp.float32
BF16 = jnp.bfloat16
I32 = jnp.int32

LANES = 128
SUBLANES = 8
VMEM_LIMIT = 56 * 1024 * 1024

D_MODEL = 1024
HEAD_DIM = 64
PAGE_SIZE = 128
IDX_HEADS = 8
N_EXPERTS = 32
TOP_K = 4
TOPK_MAX = 256
ROPE_DIM = 16
ROPE_THETA = 500000.0
GLA_GATE_TAU = 16.0
SWIGLU_ALPHA = 1.702
SWIGLU_LIMIT = 7.0
LN_EPS = 1e-5
DEPTH = 4
DEEPNORM_ALPHA = (2 * DEPTH) ** 0.25

_OFF = dict(gq=0, gk=256, gv=512, gr=768, glr=1024, hq=1040, hf=1296, hi=1552, hgt=1808,
            dq=2064, dk=2576, dv=2832, iq=3088, iw=3600, ik=3608)
PROJ_W = 3712
MISC_IW = 80
MISC_GLR = 88
NEG_BIG = -1e30

GLA_CHUNK = 128
GLA_SUB = 16
GLA_EXP_CLAMP = 60.0

MOE_BLK = 256
DMA_UNROLL = 8
DMA_WAIT_UNROLL = 32


def _cparams(sem, vmem=VMEM_LIMIT):
    return pltpu.CompilerParams(dimension_semantics=sem, vmem_limit_bytes=vmem)


def _dot(a, b, precision=None):
    return jnp.dot(a, b, preferred_element_type=F32, precision=precision)


def _dot_nt(a, b, precision=None):
    return lax.dot_general(a, b, (((1,), (1,)), ((), ())), preferred_element_type=F32,
                           precision=precision)


def _sigmoid(x):
    return 1.0 / (1.0 + jnp.exp(-x))


def _log_sigmoid(x):
    return jnp.minimum(x, 0.0) - jnp.log(1.0 + jnp.exp(-jnp.abs(x)))


def _layer_norm_rows(x, g, b):
    mu = jnp.mean(x, axis=-1, keepdims=True)
    xc = x - mu
    var = jnp.mean(xc * xc, axis=-1, keepdims=True)
    return xc * lax.rsqrt(var + LN_EPS) * g + b


def _lower_bound_kernel(gam_ref, lb_ref):
    g = gam_ref[...]
    m = jnp.max(g, axis=0, keepdims=True)
    e = jnp.exp(g - m)
    sm = e / jnp.sum(e, axis=0, keepdims=True)
    acc = jnp.zeros_like(sm[0:1])
    for li in range(g.shape[0]):
        acc = acc + sm[li:li + 1]
        lb_ref[li:li + 1, :] = jnp.maximum(acc - sm[0:1], 0.0)


def _lower_bounds(hg_gamma):
    return pl.pallas_call(
        _lower_bound_kernel, out_shape=jax.ShapeDtypeStruct(hg_gamma.shape, F32),
        name="hgrn_lower_bounds")(hg_gamma.astype(F32))


def _ln_kernel(x_ref, g_ref, b_ref, o_ref):
    o_ref[...] = _layer_norm_rows(x_ref[...], g_ref[...], b_ref[...])


def _input_layer_norm(x, g, b, tm):
    n, d = x.shape
    return pl.pallas_call(
        _ln_kernel, grid=(n // tm,),
        in_specs=[pl.BlockSpec((tm, d), lambda i: (i, 0)),
                  pl.BlockSpec((1, d), lambda i: (0, 0)),
                  pl.BlockSpec((1, d), lambda i: (0, 0))],
        out_specs=pl.BlockSpec((tm, d), lambda i: (i, 0)),
        out_shape=jax.ShapeDtypeStruct((n, d), F32),
        compiler_params=_cparams(("parallel",)), name="input_ln")(x, g.reshape(1, d), b.reshape(1, d))


def _inproj_kernel(h_ref, w_ref, cos_ref, sa_ref, sb_ref,
                   g_ref, hh_ref, misc_ref, q_ref, qi_ref, k_ref, v_ref, ki_ref,
                   kb_ref, vb_ref, kid_ref):
    hb = h_ref[...].astype(BF16)
    cos, sa, sb = cos_ref[...], sa_ref[...], sb_ref[...]

    def proj(lo, width):
        return _dot(hb, w_ref[:, lo:lo + width])

    def rope(x):
        return x * cos + pltpu.roll(x, LANES - 8, 1) * sa + pltpu.roll(x, 8, 1) * sb

    g_ref[...] = proj(0, 1024)
    hh_ref[...] = proj(1024, 1024)
    for s in range(4):
        q_ref[:, s * LANES:(s + 1) * LANES] = (rope(proj(2048 + s * LANES, LANES)) * 0.125).astype(BF16)
        qi_ref[:, s * LANES:(s + 1) * LANES] = (rope(proj(3072 + s * LANES, LANES)) * 0.125).astype(BF16)
    for s in range(2):
        kr = rope(proj(2560 + s * LANES, LANES))
        k_ref[:, s * LANES:(s + 1) * LANES] = kr
        kb_ref[:, s * LANES:(s + 1) * LANES] = kr.astype(BF16)
    vv = proj(2816, 256)
    v_ref[...] = vv
    vb_ref[...] = vv.astype(BF16)
    misc = rope(proj(3584, LANES))
    misc_ref[...] = misc
    ki_ref[...] = misc[:, 0:HEAD_DIM]
    lane = lax.broadcasted_iota(I32, misc.shape, 1)
    ki_lo = jnp.where(lane < HEAD_DIM, misc, 0.0)
    kid_ref[...] = (ki_lo + pltpu.roll(ki_lo, HEAD_DIM, 1)).astype(BF16)


def _in_projection(h, w_r, cos, sa, sb, tm):
    n = h.shape[0]
    t_blocks = cos.shape[0] // tm
    row = lambda w: pl.BlockSpec((tm, w), lambda i: (i, 0))
    tab = pl.BlockSpec((tm, LANES), lambda i: (i % t_blocks, 0))
    widths = [(1024, F32), (1024, F32), (LANES, F32), (512, BF16), (512, BF16), (256, F32), (256, F32),
              (HEAD_DIM, F32), (256, BF16), (256, BF16), (LANES, BF16)]
    return pl.pallas_call(
        _inproj_kernel, grid=(n // tm,),
        in_specs=[row(D_MODEL), pl.BlockSpec((D_MODEL, PROJ_W), lambda i: (0, 0)), tab, tab, tab],
        out_specs=[row(w) for w, _ in widths],
        out_shape=[jax.ShapeDtypeStruct((n, w), dt) for w, dt in widths],
        compiler_params=_cparams(("parallel",)), name="in_projection")(h, w_r, cos, sa, sb)


def _gla_kernel(g_ref, h_ref, misc_ref, wg_ref, bg_ref, lb_ref, gn_ref, hn_ref, s0_ref,
                o_ref, sT_ref, state, *, n_chunks, t_valid):
    c_idx = pl.program_id(1)

    @pl.when(c_idx == 0)
    def _():
        state[...] = s0_ref[0]

    ch = GLA_CHUNK
    row = lax.broadcasted_iota(I32, (ch, ch), 0)
    col = lax.broadcasted_iota(I32, (ch, ch), 1)
    tri = jnp.where(col <= row, 1.0, 0.0).astype(F32)
    lane = lax.broadcasted_iota(I32, (ch, LANES), 1)
    bd = jnp.where((row < HEAD_DIM) == (col < HEAD_DIM), 1.0, 0.0).astype(F32)
    head_mean = bd * (1.0 / HEAD_DIM)
    lane32 = lax.broadcasted_iota(I32, (2 * GLA_SUB, LANES), 1)
    row32 = lax.broadcasted_iota(I32, (2 * GLA_SUB, LANES), 0)
    head_sel = jnp.where((row32 < GLA_SUB) == (lane32 < HEAD_DIM), 1.0, 0.0).astype(F32)
    lane16 = lax.broadcasted_iota(I32, (GLA_SUB, LANES), 1)

    def chunk(ci, carry):
        r0 = pl.multiple_of(ci * ch, ch)
        rows = pl.ds(r0, ch)
        gx = g_ref[rows, :]
        hx = h_ref[rows, :]
        z = _dot(misc_ref[rows, :], wg_ref[...], precision=lax.Precision.HIGHEST) + bg_ref[...]
        lg_gla = _log_sigmoid(z) * (1.0 / GLA_GATE_TAU)
        lb = lb_ref[...]
        zf = hx[:, 256:512]
        a_ = jnp.log(lb)
        c_ = jnp.log(1.0 - lb) + _log_sigmoid(zf)
        lg_hg = jnp.maximum(a_, c_) + jnp.log(1.0 + jnp.exp(-jnp.abs(a_ - c_)))
        k_hg = (1.0 - lb) * _sigmoid(-zf)
        hq = hx[:, 0:256]
        q_hg = hq * _sigmoid(hq)
        gr = gx[:, 768:1024]
        mix = ((gx[:, 0:256] * (HEAD_DIM ** -0.5), gx[:, 256:512], gx[:, 512:768], lg_gla,
                gr * _sigmoid(gr), gn_ref[...]),
               (q_hg, k_hg, hx[:, 512:768], lg_hg, _sigmoid(hx[:, 768:1024]), hn_ref[...]))
        if t_valid is not None:
            tpos = c_idx * (n_chunks * ch) + r0 + lax.broadcasted_iota(I32, (ch, 1), 0)
            live = tpos < t_valid
        for mi, (qm, km, vm, lgm, gatem, normm) in enumerate(mix):
            for p in range(2):
                sl = slice(p * LANES, (p + 1) * LANES)
                q2, k2, v2, lg2 = qm[:, sl], km[:, sl], vm[:, sl], lgm[:, sl]
                if t_valid is not None:
                    k2 = jnp.where(live, k2, 0.0)
                    lg2 = jnp.where(live, lg2, 0.0)
                b = _dot(tri, lg2, precision=lax.Precision.HIGHEST)
                b_last = b[ch - 1:ch, :]
                sT = state[2 * mi + p]
                o = _dot_nt((q2 * jnp.exp(b)).astype(BF16), sT.astype(BF16))
                v2b = v2.astype(BF16)
                parts = []
                for i in range(ch // GLA_SUB):
                    lo = i * GLA_SUB
                    n = lo + GLA_SUB
                    ref = b[lo - 1:lo, :] if i > 0 else jnp.zeros((1, LANES), F32)
                    qs = q2[lo:n, :] * jnp.exp(b[lo:n, :] - ref)
                    ks = k2[0:n, :] * jnp.exp(jnp.minimum(ref - b[0:n, :], GLA_EXP_CLAMP))
                    lhs = (jnp.concatenate([qs, qs], axis=0) * head_sel).astype(BF16)
                    a = _dot_nt(lhs, ks.astype(BF16))
                    tt = lo + (lax.broadcasted_iota(I32, (2 * GLA_SUB, n), 0) % GLA_SUB)
                    ss = lax.broadcasted_iota(I32, (2 * GLA_SUB, n), 1)
                    a = jnp.where(ss <= tt, a, 0.0)
                    oi = _dot(a.astype(BF16), v2b[0:n, :])
                    parts.append(jnp.where(lane16 < HEAD_DIM, oi[0:GLA_SUB, :], oi[GLA_SUB:, :]))
                o = o + jnp.concatenate(parts, axis=0)
                kd = k2 * jnp.exp(b_last - b)
                upd = _dot(v2.T.astype(BF16), kd.astype(BF16))
                state[2 * mi + p] = (sT * jnp.exp(b_last) + upd) * bd
                ms = _dot(o * o, head_mean, precision=lax.Precision.HIGHEST)
                on = o * lax.rsqrt(ms + LN_EPS) * normm * gatem[:, sl]
                o_ref[rows, mi * 256 + p * LANES:mi * 256 + (p + 1) * LANES] = on
        return carry

    lax.fori_loop(0, n_chunks, chunk, 0)

    @pl.when(c_idx == pl.num_programs(1) - 1)
    def _():
        sT_ref[0] = state[...]


def _gla_hgrn(g_all, h_all, misc, wg_pad, bg, lb, gn, hn, s0_bd, batch, t_pad, t_valid):
    ct = min(2 * GLA_CHUNK, t_pad)
    steps = t_pad // ct
    n = batch * t_pad
    row = lambda w: pl.BlockSpec((ct, w), lambda b, c: (b * steps + c, 0))
    const = lambda shp: pl.BlockSpec(shp, lambda b, c: tuple(0 for _ in shp))
    st = pl.BlockSpec((1, 4, LANES, LANES), lambda b, c: (b, 0, 0, 0))
    kern = functools.partial(_gla_kernel, n_chunks=ct // GLA_CHUNK,
                             t_valid=None if t_valid == t_pad else t_valid)
    return pl.pallas_call(
        kern, grid=(batch, steps),
        in_specs=[row(1024), row(1024), row(LANES), const((LANES, 256)), const((1, 256)), const((1, 256)),
                  const((1, LANES)), const((1, LANES)), st],
        out_specs=[row(512), st],
        out_shape=[jax.ShapeDtypeStruct((n, 512), F32),
                   jax.ShapeDtypeStruct((batch, 4, LANES, LANES), F32)],
        scratch_shapes=[pltpu.VMEM((4, LANES, LANES), F32)],
        compiler_params=_cparams(("parallel", "arbitrary")), name="gla_hgrn")(
            g_all, h_all, misc, wg_pad, bg, lb, gn, hn, s0_bd)


def _sort_key(x):
    bits = lax.bitcast_convert_type(x, I32)
    return jnp.where(bits < 0, bits ^ jnp.int32(0x7FFFFFFF), bits)


def _kth_largest_key(count_ge, shape, k):
    int_min = jnp.int32(-2 ** 31)

    def body(it, cand):
        trial = cand | jnp.left_shift(jnp.int32(1), 31 - it)
        cnt = count_ge(trial ^ int_min)
        return jnp.where(cnt >= k, trial, cand)

    cand = lax.fori_loop(0, 32, body, jnp.zeros(shape, I32))
    return cand ^ int_min


def _dsa_prompt_kernel(q_ref, qi_ref, misc_ref, kb_ref, vb_ref, kid_ref, o_ref,
                       sc_ref, key_ref, neg_ref, *, seq, kt, topk):
    qb = q_ref.shape[0]
    i = pl.program_id(1)
    q_lo = i * qb
    n_t = seq // kt
    n_act = (q_lo + qb - 1) // kt + 1
    row_pos = q_lo + lax.broadcasted_iota(I32, (qb, 1), 0)
    lane_q = lax.broadcasted_iota(I32, (qb, LANES), 1)
    half = (lane_q < HEAD_DIM, lane_q >= HEAD_DIM)
    w8 = misc_ref[...][:, MISC_IW:MISC_IW + IDX_HEADS] * (IDX_HEADS ** -0.5)

    sc_ref[...] = jnp.full(sc_ref.shape, -jnp.inf, F32)
    for t in range(n_t):
        @pl.when(t < n_act)
        def _(t=t):
            cols = slice(t * kt, (t + 1) * kt)
            kid = kid_ref[cols, :]
            acc = jnp.zeros((qb, kt), F32)
            for j in range(IDX_HEADS // 2):
                slab = qi_ref[:, j * LANES:(j + 1) * LANES]
                for c in range(2):
                    lhs = jnp.where(half[c], slab, jnp.zeros_like(slab))
                    logit = _dot_nt(lhs, kid)
                    acc = acc + jnp.maximum(logit, 0.0) * w8[:, 2 * j + c:2 * j + c + 1]
            col_pos = t * kt + lax.broadcasted_iota(I32, (1, kt), 1)
            sc_ref[:, cols] = jnp.where(col_pos <= row_pos, acc, -jnp.inf)
    key_ref[...] = _sort_key(sc_ref[...])

    for nt in range(1, n_t + 1):
        @pl.when(n_act == nt)
        def _(nt=nt):
            w = nt * kt

            def count_ge(thr):
                return jnp.sum(jnp.where(key_ref[:, 0:w] >= thr, 1.0, 0.0), axis=1, keepdims=True)

            thr = _kth_largest_key(count_ge, (qb, 1), float(topk))
            col_pos = lax.broadcasted_iota(I32, (1, w), 1)
            keep = (key_ref[:, 0:w] >= thr) & (col_pos <= row_pos)
            neg_ref[:, 0:w] = jnp.where(keep, 0.0, NEG_BIG)
            for p in range(2):
                kp = kb_ref[0:w, p * LANES:(p + 1) * LANES]
                vp = vb_ref[0:w, p * LANES:(p + 1) * LANES]
                for g in range(2):
                    s = 2 * p + g
                    qs = q_ref[:, s * LANES:(s + 1) * LANES]
                    outs = []
                    for c in range(2):
                        lhs = jnp.where(half[c], qs, jnp.zeros_like(qs))
                        sc = _dot_nt(lhs, kp) + neg_ref[:, 0:w]
                        m = jnp.max(sc, axis=1, keepdims=True)
                        pe = jnp.exp(sc - m)
                        l = jnp.sum(pe, axis=1, keepdims=True)
                        outs.append(_dot(pe.astype(BF16), vp) / l)
                    o_ref[:, s * LANES:(s + 1) * LANES] = jnp.where(half[0], outs[0], outs[1])


def _dsa_prompt(q, qi, misc, kb, vb, kid, batch, seq, qb=128):
    n = batch * seq
    nq = seq // qb
    kt = min(512, seq)
    topk = min(TOPK_MAX, seq // 4)
    rowq = lambda w: pl.BlockSpec((qb, w), lambda b, i: (b * nq + i, 0))
    full = lambda w: pl.BlockSpec((seq, w), lambda b, i: (b, 0))
    kern = functools.partial(_dsa_prompt_kernel, seq=seq, kt=kt, topk=topk)
    return pl.pallas_call(
        kern, grid=(batch, nq),
        in_specs=[rowq(512), rowq(512), rowq(LANES), full(256), full(256), full(LANES)],
        out_specs=rowq(512),
        out_shape=jax.ShapeDtypeStruct((n, 512), F32),
        scratch_shapes=[pltpu.VMEM((qb, seq), F32), pltpu.VMEM((qb, seq), I32), pltpu.VMEM((qb, seq), F32)],
        compiler_params=_cparams(("parallel", "arbitrary")), name="dsa_prompt")(q, qi, misc, kb, vb, kid)


PAGES_PER_STEP = 8
DSA_Q_HEADS = 8
DSA_KV_HEADS = 4


def _dsa_sample_score_kernel(pt_ref, qi_ref, w_ref, *refs):
    pages, out_ref = refs[:PAGES_PER_STEP], refs[PAGES_PER_STEP]
    qi = qi_ref[0]
    w = w_ref[0] * (IDX_HEADS ** -0.5)
    for j, pg in enumerate(pages):
        logit = _dot(qi, pg[...].astype(BF16))
        out_ref[0, j:j + 1, :] = jnp.sum(jnp.maximum(logit, 0.0) * w, axis=0, keepdims=True)


def _dsa_sample_scores(page_table, cache_kidx_t, li, qi3, w3):
    bd, n_pages = page_table.shape
    groups = n_pages // PAGES_PER_STEP

    def page_spec(j):
        return pl.BlockSpec((None, None, HEAD_DIM, PAGE_SIZE),
                            lambda b, g, pt: (li, pt[b, g * PAGES_PER_STEP + j], 0, 0))

    grid_spec = pltpu.PrefetchScalarGridSpec(
        num_scalar_prefetch=1, grid=(bd, groups),
        in_specs=[pl.BlockSpec((1, IDX_HEADS, HEAD_DIM), lambda b, g, pt: (b, 0, 0)),
                  pl.BlockSpec((1, IDX_HEADS, 1), lambda b, g, pt: (b, 0, 0))]
        + [page_spec(j) for j in range(PAGES_PER_STEP)],
        out_specs=pl.BlockSpec((1, PAGES_PER_STEP, PAGE_SIZE), lambda b, g, pt: (b, g, 0)))
    return pl.pallas_call(
        _dsa_sample_score_kernel, grid_spec=grid_spec,
        out_shape=jax.ShapeDtypeStruct((bd, n_pages, PAGE_SIZE), F32),
        compiler_params=_cparams(("parallel", "arbitrary")), name="dsa_sample_scores")(
            page_table, qi3, w3, *([cache_kidx_t] * PAGES_PER_STEP))


def _dsa_sample_threshold_kernel(sc_ref, qi_ref, ki_ref, w_ref, thr_ref, self_ref, key_ref, *, topk):
    qi = qi_ref[...].astype(F32)
    ki = ki_ref[...].astype(BF16).astype(F32)
    w = w_ref[...] * (IDX_HEADS ** -0.5)
    logit = jnp.sum(qi * ki, axis=2, keepdims=True)
    self_sc = jnp.sum(jnp.maximum(logit, 0.0) * w, axis=1)
    self_key = _sort_key(self_sc)
    key_ref[...] = _sort_key(sc_ref[...])

    def count_ge(thr):
        c = jnp.sum(jnp.where(key_ref[...] >= thr, 1.0, 0.0), axis=1, keepdims=True)
        return c + jnp.where(self_key >= thr, 1.0, 0.0)

    thr = _kth_largest_key(count_ge, self_key.shape, float(topk))
    thr_ref[...] = thr
    self_ref[...] = jnp.where(self_key >= thr, 1.0, 0.0)


def _dsa_sample_threshold(scores2, qi3, ki3, w3, topk):
    bd, past = scores2.shape
    kern = functools.partial(_dsa_sample_threshold_kernel, topk=topk)
    return pl.pallas_call(
        kern,
        out_shape=[jax.ShapeDtypeStruct((bd, 1), I32), jax.ShapeDtypeStruct((bd, 1), F32)],
        scratch_shapes=[pltpu.VMEM((bd, past), I32)],
        compiler_params=pltpu.CompilerParams(vmem_limit_bytes=VMEM_LIMIT),
        name="dsa_sample_threshold")(scores2, qi3, ki3, w3)


def _dsa_sample_attn_kernel(pt_ref, qbd_ref, sc_ref, thr_ref, self_ref, kn_ref, vn_ref, *refs,
                            n_groups):
    kpages = refs[:PAGES_PER_STEP]
    vpages = refs[PAGES_PER_STEP:2 * PAGES_PER_STEP]
    o_ref, m_ref, l_ref, acc_ref = refs[2 * PAGES_PER_STEP:]
    g = pl.program_id(1)
    nh = qbd_ref.shape[1]

    @pl.when(g == 0)
    def _():
        m_ref[...] = jnp.full(m_ref.shape, NEG_BIG, F32)
        l_ref[...] = jnp.zeros(l_ref.shape, F32)
        acc_ref[...] = jnp.zeros(acc_ref.shape, F32)

    qbd = qbd_ref[0]
    thr = thr_ref[0]

    for j in range(PAGES_PER_STEP):
        keep = _sort_key(sc_ref[0, j:j + 1, :]) >= thr
        sc = jnp.where(keep, _dot(qbd, kpages[j][...].astype(BF16)), NEG_BIG)
        m_old = m_ref[...]
        m_new = jnp.maximum(m_old, jnp.max(sc, axis=1, keepdims=True))
        alpha = jnp.exp(m_old - m_new)
        pe = jnp.where(keep, jnp.exp(sc - m_new), 0.0)
        l_ref[...] = alpha * l_ref[...] + jnp.sum(pe, axis=1, keepdims=True)
        acc_ref[...] = alpha * acc_ref[...] + _dot_nt(pe.astype(BF16), vpages[j][...].astype(BF16))
        m_ref[...] = m_new

    @pl.when(g == n_groups - 1)
    def _():
        kn = kn_ref[0].astype(BF16).astype(F32)
        sc = jnp.sum(qbd.astype(F32) * kn, axis=1, keepdims=True)
        sc = jnp.where(self_ref[0] > 0.5, sc, NEG_BIG)
        m_old = m_ref[...]
        m_new = jnp.maximum(m_old, sc)
        alpha = jnp.exp(m_old - m_new)
        pe = jnp.exp(sc - m_new)
        pe = jnp.where(sc > 0.5 * NEG_BIG, pe, 0.0)
        l = alpha * l_ref[...] + pe
        acc = alpha * acc_ref[...] + pe * vn_ref[0].astype(BF16).astype(F32)
        res = acc / l
        kvh = lax.broadcasted_iota(I32, (nh, HEAD_DIM), 0) // 2
        out = jnp.zeros((nh, HEAD_DIM), F32)
        for c in range(4):
            out = out + jnp.where(kvh == c, res[:, c * HEAD_DIM:(c + 1) * HEAD_DIM], 0.0)
        o_ref[0] = out


def _dsa_sample_attention(page_table, cache_k2, cache_v2, li, qbd, scores3, thr3, self3, kn3, vn3):
    bd, n_pages = page_table.shape
    groups = n_pages // PAGES_PER_STEP
    kvw = cache_k2.shape[-2]

    def page_spec(j):
        return pl.BlockSpec((None, None, kvw, PAGE_SIZE),
                            lambda b, g, pt: (li, pt[b, g * PAGES_PER_STEP + j], 0, 0))

    per_b = lambda shp: pl.BlockSpec((1,) + shp, lambda b, g, pt: (b, 0, 0))
    grid_spec = pltpu.PrefetchScalarGridSpec(
        num_scalar_prefetch=1, grid=(bd, groups),
        in_specs=[per_b((DSA_Q_HEADS, kvw)),
                  pl.BlockSpec((1, PAGES_PER_STEP, PAGE_SIZE), lambda b, g, pt: (b, g, 0)),
                  per_b((1, 1)), per_b((1, 1)), per_b((1, kvw)), per_b((1, kvw))]
        + [page_spec(j) for j in range(PAGES_PER_STEP)] * 2,
        out_specs=per_b((DSA_Q_HEADS, HEAD_DIM)),
        scratch_shapes=[pltpu.VMEM((DSA_Q_HEADS, 1), F32), pltpu.VMEM((DSA_Q_HEADS, 1), F32),
                        pltpu.VMEM((DSA_Q_HEADS, kvw), F32)])
    kern = functools.partial(_dsa_sample_attn_kernel, n_groups=groups)
    return pl.pallas_call(
        kern, grid_spec=grid_spec,
        out_shape=jax.ShapeDtypeStruct((bd, DSA_Q_HEADS, HEAD_DIM), F32),
        compiler_params=_cparams(("parallel", "arbitrary")), name="dsa_sample_attention")(
            page_table, qbd, scores3, thr3, self3, kn3, vn3,
            *([cache_k2] * PAGES_PER_STEP), *([cache_v2] * PAGES_PER_STEP))


def _post_mix_kernel(oa_ref, od_ref, x_ref, p_ref, woa_ref, wod_ref, g1_ref, b1_ref, wr_ref, br_ref,
                     wpg_ref, wpp_ref, cnt0_ref, x1t_ref, base_ref, route_ref, cnt_ref):
    tb = x_ref.shape[0]

    @pl.when(pl.program_id(0) == 0)
    def _():
        cnt_ref[...] = cnt0_ref[...]

    mixv = _dot(oa_ref[...].astype(BF16), woa_ref[...]) + _dot(od_ref[...].astype(BF16), wod_ref[...])
    x1 = _layer_norm_rows(DEEPNORM_ALPHA * x_ref[...] + mixv, g1_ref[...], b1_ref[...])
    x1b = x1.astype(BF16)
    for j in range(SUBLANES):
        x1t_ref[pl.ds(j, tb, stride=SUBLANES), :] = x1[:, j * LANES:(j + 1) * LANES]
    ple = _sigmoid(_dot(x1b, wpg_ref[...])) * _dot(p_ref[...].astype(BF16), wpp_ref[...])
    base_ref[...] = DEEPNORM_ALPHA * x1 + ple

    logits = _dot(x1, wr_ref[...], precision=lax.Precision.HIGHEST) + br_ref[...]
    lane = lax.broadcasted_iota(I32, (tb, LANES), 1)
    lanef = lane.astype(F32)
    vals, hots = [], []
    work = logits
    for _ in range(TOP_K):
        m = jnp.max(work, axis=1, keepdims=True)
        idx = jnp.min(jnp.where(work == m, lanef, float(LANES)), axis=1, keepdims=True)
        hot = lanef == idx
        vals.append(m)
        hots.append(hot)
        work = jnp.where(hot, -jnp.inf, work)
    exps = [jnp.exp(v - vals[0]) for v in vals]
    den = exps[0] + exps[1] + exps[2] + exps[3]
    oh = jnp.zeros((tb, LANES), F32)
    for hot in hots:
        oh = oh + jnp.where(hot, 1.0, 0.0)
    r = lax.broadcasted_iota(I32, (tb, tb), 0)
    c = lax.broadcasted_iota(I32, (tb, tb), 1)
    strict = jnp.where(c < r, 1.0, 0.0).astype(BF16)
    before = _dot(strict, oh.astype(BF16)) + cnt_ref[...]
    route = jnp.zeros((tb, LANES), F32)
    for k in range(TOP_K):
        gate = exps[k] / den
        eidx = jnp.sum(jnp.where(hots[k], lanef, 0.0), axis=1, keepdims=True)
        rank = jnp.sum(jnp.where(hots[k], before, 0.0), axis=1, keepdims=True)
        route = route + jnp.where(lane == k, gate, 0.0) + jnp.where(lane == TOP_K + k, eidx, 0.0) \
            + jnp.where(lane == 2 * TOP_K + k, rank, 0.0)
    route_ref[...] = route
    cnt_ref[...] = cnt_ref[...] + jnp.sum(oh, axis=0, keepdims=True)


def _post_mix(oa, od, x, p, woa, wod, g1, b1, wr, br, wpg, wpp, cnt0, tb):
    n = x.shape[0]
    row = lambda w: pl.BlockSpec((tb, w), lambda i: (i, 0))
    const = lambda a: pl.BlockSpec(a.shape, lambda i: tuple(0 for _ in a.shape))
    return pl.pallas_call(
        _post_mix_kernel, grid=(n // tb,),
        in_specs=[row(512), row(512), row(D_MODEL), row(p.shape[1]), const(woa), const(wod), const(g1),
                  const(b1), const(wr), const(br), const(wpg), const(wpp), const(cnt0)],
        out_specs=[pl.BlockSpec((tb * SUBLANES, LANES), lambda i: (i, 0)), row(D_MODEL), row(LANES),
                   pl.BlockSpec((1, LANES), lambda i: (0, 0))],
        out_shape=[jax.ShapeDtypeStruct((n * SUBLANES, LANES), F32), jax.ShapeDtypeStruct((n, D_MODEL), F32),
                   jax.ShapeDtypeStruct((n, LANES), F32), jax.ShapeDtypeStruct((1, LANES), F32)],
        compiler_params=_cparams(("arbitrary",)), name="post_mix_router")(
            oa, od, x, p, woa, wod, g1, b1, wr, br, wpg, wpp, cnt0)


def _row_copy(src, src_row, dst, dst_row, sem):
    return pltpu.make_async_copy(src.at[pl.ds(src_row * SUBLANES, SUBLANES), :],
                                 dst.at[pl.ds(dst_row * SUBLANES, SUBLANES), :], sem)


def _moe_scatter_kernel(dest_ref, x_ref, xs_in_ref, xs_ref, sem):
    del xs_in_ref
    n_copies = dest_ref.shape[-1]

    def issue(c, carry):
        for u in range(DMA_UNROLL):
            a = c * DMA_UNROLL + u
            _row_copy(x_ref, a // TOP_K, xs_ref, dest_ref[0, 0, a], sem).start()
        return carry

    lax.fori_loop(0, n_copies // DMA_UNROLL, issue, 0)
    _drain_row_copies(x_ref, xs_ref, sem, n_copies)


def _drain_row_copies(src, dst, sem, n_copies):
    def drain(c, carry):
        for _ in range(DMA_WAIT_UNROLL):
            _row_copy(src, 0, dst, 0, sem).wait()
        return carry

    lax.fori_loop(0, n_copies // DMA_WAIT_UNROLL, drain, 0)
    for _ in range(n_copies % DMA_WAIT_UNROLL):
        _row_copy(src, 0, dst, 0, sem).wait()


def _moe_scatter(dest3, x1t, n_slots, tb):
    n = x1t.shape[0] // SUBLANES
    zeros = jnp.zeros((n_slots * SUBLANES, LANES), F32)
    return pl.pallas_call(
        _moe_scatter_kernel, grid=(n // tb,),
        in_specs=[pl.BlockSpec((1, 1, tb * TOP_K), lambda i: (i, 0, 0), memory_space=pltpu.SMEM),
                  pl.BlockSpec((tb * SUBLANES, LANES), lambda i: (i, 0)),
                  pl.BlockSpec(memory_space=pl.ANY)],
        out_specs=pl.BlockSpec(memory_space=pl.ANY),
        out_shape=jax.ShapeDtypeStruct(zeros.shape, F32),
        scratch_shapes=[pltpu.SemaphoreType.DMA],
        input_output_aliases={2: 0},
        compiler_params=_cparams(("arbitrary",)), name="moe_scatter")(dest3, x1t, zeros)


def _moe_expert_kernel(be_ref, nu_ref, xs_ref, wg_ref, bg_ref, wu_ref, bu_ref, wd_ref, bd_ref, ys_ref,
                       wgb, wub, wdb):
    i = pl.program_id(0)
    blk = xs_ref.shape[0] // SUBLANES

    @pl.when(i < nu_ref[0])
    def _():
        prev = be_ref[jnp.maximum(i - 1, 0)]

        @pl.when((i == 0) | (prev != be_ref[i]))
        def _():
            wgb[...] = wg_ref[...].astype(BF16)
            wub[...] = wu_ref[...].astype(BF16)
            wdb[...] = wd_ref[...].astype(BF16)

        x = jnp.concatenate([xs_ref[pl.ds(j, blk, stride=SUBLANES), :] for j in range(SUBLANES)],
                            axis=1).astype(BF16)
        g = jnp.minimum(_dot(x, wgb[...]) + bg_ref[...], SWIGLU_LIMIT)
        u = jnp.clip(_dot(x, wub[...]) + bu_ref[...], -SWIGLU_LIMIT, SWIGLU_LIMIT)
        hdn = (u + 1.0) * g * _sigmoid(SWIGLU_ALPHA * g)
        y = _dot(hdn.astype(BF16), wdb[...]) + bd_ref[...]
        for j in range(SUBLANES):
            ys_ref[pl.ds(j, blk, stride=SUBLANES), :] = y[:, j * LANES:(j + 1) * LANES]

    @pl.when(i >= nu_ref[0])
    def _():
        ys_ref[...] = jnp.zeros(ys_ref.shape, F32)


def _moe_experts(block_expert, n_used, xs, li, wg, bg, wu, bu, wd, bd):
    n_blocks = block_expert.shape[0]
    d, f = wg.shape[2], wg.shape[3]

    def blk_idx(i, be, nu):
        return (jnp.minimum(i, nu[0] - 1), 0)

    def w_idx(i, be, nu):
        return (li, be[jnp.minimum(i, nu[0] - 1)], 0, 0)

    rows = pl.BlockSpec((MOE_BLK * SUBLANES, LANES), blk_idx)
    out_rows = pl.BlockSpec((MOE_BLK * SUBLANES, LANES), lambda i, be, nu: (i, 0))
    grid_spec = pltpu.PrefetchScalarGridSpec(
        num_scalar_prefetch=2, grid=(n_blocks,),
        in_specs=[rows,
                  pl.BlockSpec((None, None, d, f), w_idx), pl.BlockSpec((None, None, 1, f), w_idx),
                  pl.BlockSpec((None, None, d, f), w_idx), pl.BlockSpec((None, None, 1, f), w_idx),
                  pl.BlockSpec((None, None, f, d), w_idx), pl.BlockSpec((None, None, 1, d), w_idx)],
        out_specs=out_rows,
        scratch_shapes=[pltpu.VMEM((d, f), BF16), pltpu.VMEM((d, f), BF16), pltpu.VMEM((f, d), BF16)])
    return pl.pallas_call(
        _moe_expert_kernel, grid_spec=grid_spec,
        out_shape=jax.ShapeDtypeStruct(xs.shape, F32),
        compiler_params=_cparams(("arbitrary",)), name="moe_experts")(
            block_expert, n_used, xs, wg, bg, wu, bu, wd, bd)


def _moe_combine_kernel(dest_ref, ys_ref, route_ref, base_ref, g2_ref, b2_ref, o_ref, yg, sem):
    tb = base_ref.shape[0]
    n_copies = tb * TOP_K

    def issue(c, carry):
        for u in range(DMA_UNROLL):
            a = c * DMA_UNROLL + u
            n, k = c * (DMA_UNROLL // TOP_K) + u // TOP_K, u % TOP_K
            _row_copy(ys_ref, dest_ref[0, 0, a], yg, k * tb + n, sem).start()
        return carry

    lax.fori_loop(0, n_copies // DMA_UNROLL, issue, 0)
    _drain_row_copies(ys_ref, yg, sem, n_copies)

    gates = route_ref[...][:, 0:TOP_K]
    cols = []
    for j in range(SUBLANES):
        acc = jnp.zeros((tb, LANES), F32)
        for k in range(TOP_K):
            acc = acc + gates[:, k:k + 1] * yg[pl.ds(k * tb * SUBLANES + j, tb, stride=SUBLANES), :]
        cols.append(acc)
    ffn = jnp.concatenate(cols, axis=1)
    o_ref[...] = _layer_norm_rows(base_ref[...] + ffn, g2_ref[...], b2_ref[...])


def _moe_combine(dest3, ys, route, base, g2, b2, tb):
    n = base.shape[0]
    row = lambda w: pl.BlockSpec((tb, w), lambda i: (i, 0))
    const = lambda a: pl.BlockSpec(a.shape, lambda i: tuple(0 for _ in a.shape))
    return pl.pallas_call(
        _moe_combine_kernel, grid=(n // tb,),
        in_specs=[pl.BlockSpec((1, 1, tb * TOP_K), lambda i: (i, 0, 0), memory_space=pltpu.SMEM),
                  pl.BlockSpec(memory_space=pl.ANY), row(LANES), row(D_MODEL), const(g2), const(b2)],
        out_specs=row(D_MODEL),
        out_shape=jax.ShapeDtypeStruct((n, D_MODEL), F32),
        scratch_shapes=[pltpu.VMEM((TOP_K * tb * SUBLANES, LANES), F32), pltpu.SemaphoreType.DMA],
        compiler_params=_cparams(("arbitrary",)), name="moe_combine")(dest3, ys, route, base, g2, b2)


def _moe_layer(x1t, route, cnt, base, lw, ew, li, tb):
    n = base.shape[0]
    a = n * TOP_K
    n_blocks = a // MOE_BLK + N_EXPERTS
    counts = cnt[0, :N_EXPERTS].astype(I32)
    padded = (counts + MOE_BLK - 1) // MOE_BLK * MOE_BLK
    pad_end = jnp.cumsum(padded)
    pad_start = pad_end - padded
    top_idx = route[:, TOP_K:2 * TOP_K].astype(I32)
    rank = route[:, 2 * TOP_K:3 * TOP_K].astype(I32)
    dest = pad_start[top_idx] + rank
    dest3 = dest.reshape(n // tb, 1, tb * TOP_K)
    block_start = jnp.arange(n_blocks, dtype=I32) * MOE_BLK
    block_expert = jnp.minimum(jnp.sum((pad_end[None, :] <= block_start[:, None]).astype(I32), axis=1),
                               N_EXPERTS - 1).astype(I32)
    n_used = (pad_end[-1:] // MOE_BLK).astype(I32)
    xs = _moe_scatter(dest3, x1t, n_blocks * MOE_BLK, tb)
    ys = _moe_experts(block_expert, n_used, xs, li, *ew)
    return _moe_combine(dest3, ys, route, base, lw['ln2_g'], lw['ln2_b'], tb)


_DQ_PROMPT_ORDER = (0, 2, 1, 3, 4, 6, 5, 7)


def _relayout_w_in(w, head_order):
    col = lambda name, width: w[:, _OFF[name]:_OFF[name] + width]
    dq = col('dq', 512).reshape(D_MODEL, 8, HEAD_DIM)[:, jnp.array(head_order), :].reshape(D_MODEL, 512)
    z = lambda width: jnp.zeros((D_MODEL, width), w.dtype)
    misc = jnp.concatenate([col('ik', 64), z(16), col('iw', 8), col('glr', 16), z(24)], axis=1)
    out = jnp.concatenate([col('gq', 256), col('gk', 256), col('gv', 256), col('gr', 256),
                           col('hq', 256), col('hf', 256), col('hi', 256), col('hgt', 256),
                           dq, col('dk', 256), col('dv', 256), col('iq', 512), misc], axis=1)
    return out.astype(BF16)


def _rope_tables(pos):
    half = ROPE_DIM // 2
    inv = ROPE_THETA ** (-jnp.arange(half, dtype=F32) / half)
    ang = pos.astype(F32)[:, None] * inv[None, :]
    cos, sin = jnp.cos(ang), jnp.sin(ang)
    t = pos.shape[0]
    one, zero = jnp.ones((t, HEAD_DIM - ROPE_DIM), F32), jnp.zeros((t, HEAD_DIM - ROPE_DIM), F32)
    z8 = jnp.zeros((t, half), F32)
    c64 = jnp.concatenate([cos, cos, one], axis=1)
    a64 = jnp.concatenate([-sin, z8, zero], axis=1)
    b64 = jnp.concatenate([z8, sin, zero], axis=1)
    tile2 = lambda a: jnp.concatenate([a, a], axis=1)
    return tile2(c64), tile2(a64), tile2(b64)


def _layer_weights(li, w):
    wo = w['w_out'][li]
    wod = wo[512:].reshape(8, HEAD_DIM, D_MODEL)
    wg_pad = jnp.zeros((LANES, 256), F32).at[MISC_GLR:MISC_GLR + 16].set(w['w_gla_gate'][li])
    wr = jnp.zeros((D_MODEL, LANES), F32).at[:, :N_EXPERTS].set(w['w_router'][li])
    br = jnp.full((1, LANES), NEG_BIG, F32).at[0, :N_EXPERTS].set(w['b_router'][li])
    tile2 = lambda g: jnp.concatenate([g, g]).reshape(1, LANES)
    return dict(
        w_in_p=_relayout_w_in(w['w_in'][li], _DQ_PROMPT_ORDER),
        wg_pad=wg_pad, bg=w['b_gla_gate'][li].reshape(1, 256),
        gn=tile2(w['gla_norm_g'][li]), hn=tile2(w['hg_norm_g'][li]),
        woa=wo[:512].astype(BF16),
        wod_p=wod[jnp.array(_DQ_PROMPT_ORDER)].reshape(512, D_MODEL).astype(BF16),
        ln1_g=w['ln1_g'][li].reshape(1, D_MODEL), ln1_b=w['ln1_b'][li].reshape(1, D_MODEL),
        wr=wr, br=br,
        wpg=w['w_ple_gate'][li].astype(BF16), wpp=w['w_ple_proj'][li].astype(BF16),
        ln2_g=w['ln2_g'][li].reshape(1, D_MODEL), ln2_b=w['ln2_b'][li].reshape(1, D_MODEL))


def _state_to_blockdiag(s):
    b = s.shape[0]
    st = jnp.swapaxes(s, -1, -2).reshape(b, 2, 2, HEAD_DIM, HEAD_DIM)
    z = jnp.zeros_like(st[:, :, 0])
    top = jnp.concatenate([st[:, :, 0], z], axis=-1)
    bot = jnp.concatenate([z, st[:, :, 1]], axis=-1)
    return jnp.concatenate([top, bot], axis=-2)


def _state_from_blockdiag(sbd):
    b = sbd.shape[0]
    h0 = sbd[:, :, :HEAD_DIM, :HEAD_DIM]
    h1 = sbd[:, :, HEAD_DIM:, HEAD_DIM:]
    st = jnp.stack([h0, h1], axis=2).reshape(b, 4, HEAD_DIM, HEAD_DIM)
    return jnp.swapaxes(st, -1, -2)


def kernel(x_prompt, x_sample, cache_k, cache_v, cache_kidx, state_gla, state_hgrn, page_table, p_prompt, p_sample, ln_in_g, ln_in_b, w_in, w_gla_gate, b_gla_gate, gla_norm_g, hg_gamma, hg_norm_g, w_out, ln1_g, ln1_b, w_router, b_router, w_exp_gate, b_exp_gate, w_exp_up, b_exp_up, w_exp_down, b_exp_down, ln2_g, ln2_b, w_ple_gate, w_ple_proj):
    bp, t, d = x_prompt.shape
    bs, ts, _ = x_sample.shape
    assert ts == 1 and d == D_MODEL
    depth = w_in.shape[0]
    n_pages = page_table.shape[1]
    past = n_pages * PAGE_SIZE
    n_p = bp * t
    weights = dict(w_in=w_in, w_gla_gate=w_gla_gate, b_gla_gate=b_gla_gate, gla_norm_g=gla_norm_g,
                   hg_norm_g=hg_norm_g, w_out=w_out, ln1_g=ln1_g, ln1_b=ln1_b, w_router=w_router,
                   b_router=b_router, w_exp_gate=w_exp_gate, b_exp_gate=b_exp_gate, w_exp_up=w_exp_up,
                   b_exp_up=b_exp_up, w_exp_down=w_exp_down, b_exp_down=b_exp_down, ln2_g=ln2_g,
                   ln2_b=ln2_b, w_ple_gate=w_ple_gate, w_ple_proj=w_ple_proj)

    tm_p = min(512, t)
    tb_p = min(256, t)
    lbs = _lower_bounds(hg_gamma)
    tabs_p = _rope_tables(jnp.arange(t, dtype=I32))
    tabs_s = _rope_tables(jnp.full((bs,), past, I32))
    xp = _input_layer_norm(x_prompt.reshape(n_p, d), ln_in_g, ln_in_b, tm_p)
    xs = _input_layer_norm(x_sample.reshape(bs, d), ln_in_g, ln_in_b, bs)
    pages_t = lambda c: jnp.transpose(c, (0, 1, 3, 4, 2)).reshape(c.shape[0], c.shape[1], -1, PAGE_SIZE)
    cache_k2, cache_v2 = pages_t(cache_k), pages_t(cache_v)
    cache_kidx_t = jnp.transpose(cache_kidx, (0, 1, 3, 2))
    bias4 = lambda b: b.reshape(b.shape[0], b.shape[1], 1, b.shape[2])
    ew = (w_exp_gate, bias4(b_exp_gate), w_exp_up, bias4(b_exp_up), w_exp_down, bias4(b_exp_down))
    head_perm = jnp.array(_DQ_PROMPT_ORDER)
    zero_state = jnp.zeros((bp, 4, LANES, LANES), F32)
    zero_cnt = jnp.zeros((1, LANES), F32)
    topk_s = min(TOPK_MAX, (past + ts) // 4)
    head_of_row = jnp.arange(DSA_Q_HEADS)[:, None] // 2
    kv_of_lane = jnp.arange(DSA_KV_HEADS * HEAD_DIM)[None, :] // HEAD_DIM

    outs = {k: [] for k in ('kp', 'vp', 'kip', 'sgp', 'shp', 'ks', 'vs', 'kis', 'sgs', 'shs')}
    for li in range(depth):
        lw = _layer_weights(li, weights)
        lb = lbs[li].reshape(1, 256)

        (g_all, h_all, misc, q, qi, k, v, ki, kb, vb, kid) = _in_projection(xp, lw['w_in_p'], *tabs_p, tm_p)
        o_lin, st = _gla_hgrn(g_all, h_all, misc, lw['wg_pad'], lw['bg'], lb, lw['gn'], lw['hn'],
                              zero_state, bp, t, t)
        o_dsa = _dsa_prompt(q, qi, misc, kb, vb, kid, bp, t)
        x1t, base, route, cnt = _post_mix(o_lin, o_dsa, xp, p_prompt[li].reshape(n_p, -1), lw['woa'],
                                          lw['wod_p'], lw['ln1_g'], lw['ln1_b'], lw['wr'], lw['br'],
                                          lw['wpg'], lw['wpp'], zero_cnt, tb_p)
        xp = _moe_layer(x1t, route, cnt, base, lw, ew, li, tb_p)
        outs['kp'].append(k.reshape(bp, t, DSA_KV_HEADS, HEAD_DIM))
        outs['vp'].append(v.reshape(bp, t, DSA_KV_HEADS, HEAD_DIM))
        outs['kip'].append(ki.reshape(bp, t, HEAD_DIM))
        outs['sgp'].append(_state_from_blockdiag(st[:, 0:2]))
        outs['shp'].append(_state_from_blockdiag(st[:, 2:4]))

        (g_all, h_all, misc, q, qi, k, v, ki, kb, vb, kid) = _in_projection(xs, lw['w_in_p'], *tabs_s, bs)
        pad = lambda a: jnp.pad(a.reshape(bs, 1, -1), ((0, 0), (0, GLA_CHUNK - 1), (0, 0))).reshape(
            bs * GLA_CHUNK, -1)
        s0 = jnp.concatenate([_state_to_blockdiag(state_gla[li]), _state_to_blockdiag(state_hgrn[li])], axis=1)
        o_lin, st = _gla_hgrn(pad(g_all), pad(h_all), pad(misc), lw['wg_pad'], lw['bg'], lb, lw['gn'],
                              lw['hn'], s0, bs, GLA_CHUNK, 1)
        o_lin = o_lin.reshape(bs, GLA_CHUNK, -1)[:, 0]
        qi3 = qi.reshape(bs, IDX_HEADS, HEAD_DIM)
        w3 = misc[:, MISC_IW:MISC_IW + IDX_HEADS].reshape(bs, IDX_HEADS, 1)
        scores = _dsa_sample_scores(page_table, cache_kidx_t, li, qi3, w3)
        thr, self_sel = _dsa_sample_threshold(scores.reshape(bs, past), qi3, ki.reshape(bs, 1, HEAD_DIM),
                                              w3, topk_s)
        q3 = q.reshape(bs, DSA_Q_HEADS, HEAD_DIM)[:, head_perm].reshape(bs, DSA_Q_HEADS, 1, HEAD_DIM)
        qbd = jnp.where(head_of_row == kv_of_lane,
                        jnp.tile(q3, (1, 1, DSA_KV_HEADS, 1)).reshape(bs, DSA_Q_HEADS, -1), 0).astype(BF16)
        o_dsa = _dsa_sample_attention(page_table, cache_k2, cache_v2, li, qbd, scores,
                                      thr.reshape(bs, 1, 1), self_sel.reshape(bs, 1, 1),
                                      k.reshape(bs, 1, -1), v.reshape(bs, 1, -1))
        o_dsa = o_dsa[:, head_perm].reshape(bs, 512)
        x1t, base, route, cnt = _post_mix(o_lin, o_dsa, xs, p_sample[li].reshape(bs, -1), lw['woa'],
                                          lw['wod_p'], lw['ln1_g'], lw['ln1_b'], lw['wr'], lw['br'],
                                          lw['wpg'], lw['wpp'], zero_cnt, bs)
        xs = _moe_layer(x1t, route, cnt, base, lw, ew, li, bs)
        outs['ks'].append(k.reshape(bs, ts, DSA_KV_HEADS, HEAD_DIM))
        outs['vs'].append(v.reshape(bs, ts, DSA_KV_HEADS, HEAD_DIM))
        outs['kis'].append(ki.reshape(bs, ts, HEAD_DIM))
        outs['sgs'].append(_state_from_blockdiag(st[:, 0:2]))
        outs['shs'].append(_state_from_blockdiag(st[:, 2:4]))

    stack = lambda name: jnp.stack(outs[name], 0)
    return (xp.reshape(bp, t, d), xs.reshape(bs, ts, d),
            stack('kp'), stack('vp'), stack('kip'), stack('sgp'), stack('shp'),
            stack('ks'), stack('vs'), stack('kis'), stack('sgs'), stack('shs'))
```

```python
import functools

import jax
import jax.numpy as jnp
from jax import lax
from jax.experimental import pallas as pl
from jax.experimental.pallas import tpu as pltpu

F32 = jnp.float32
BF16 = jnp.bfloat16
I32 = jnp.int32

LANES = 128
SUBLANES = 8
VMEM_LIMIT = 56 * 1024 * 1024

D_MODEL = 1024
HEAD_DIM = 64
PAGE_SIZE = 128
IDX_HEADS = 8
N_EXPERTS = 32
TOP_K = 4
TOPK_MAX = 256
ROPE_DIM = 16
ROPE_THETA = 500000.0
GLA_GATE_TAU = 16.0
SWIGLU_ALPHA = 1.702
SWIGLU_LIMIT = 7.0
LN_EPS = 1e-5
DEPTH = 4
DEEPNORM_ALPHA = (2 * DEPTH) ** 0.25

_OFF = dict(gq=0, gk=256, gv=512, gr=768, glr=1024, hq=1040, hf=1296, hi=1552, hgt=1808,
            dq=2064, dk=2576, dv=2832, iq=3088, iw=3600, ik=3608)
PROJ_W = 3712
MISC_IW = 80
MISC_GLR = 88
NEG_BIG = -1e30

GLA_CHUNK = 128
GLA_SUB = 16
GLA_EXP_CLAMP = 60.0

MOE_BLK = 256
DMA_UNROLL = 8
DMA_WAIT_UNROLL = 32


def _cparams(sem, vmem=VMEM_LIMIT):
    return pltpu.CompilerParams(dimension_semantics=sem, vmem_limit_bytes=vmem)


def _dot(a, b, precision=None):
    return jnp.dot(a, b, preferred_element_type=F32, precision=precision)


def _dot_nt(a, b, precision=None):
    return lax.dot_general(a, b, (((1,), (1,)), ((), ())), preferred_element_type=F32,
                           precision=precision)


def _sigmoid(x):
    return 1.0 / (1.0 + jnp.exp(-x))


def _log_sigmoid(x):
    return jnp.minimum(x, 0.0) - jnp.log(1.0 + jnp.exp(-jnp.abs(x)))


def _layer_norm_rows(x, g, b):
    mu = jnp.mean(x, axis=-1, keepdims=True)
    xc = x - mu
    var = jnp.mean(xc * xc, axis=-1, keepdims=True)
    return xc * lax.rsqrt(var + LN_EPS) * g + b


def _lower_bound_kernel(gam_ref, lb_ref):
    g = gam_ref[...]
    m = jnp.max(g, axis=0, keepdims=True)
    e = jnp.exp(g - m)
    sm = e / jnp.sum(e, axis=0, keepdims=True)
    acc = jnp.zeros_like(sm[0:1])
    for li in range(g.shape[0]):
        acc = acc + sm[li:li + 1]
        lb_ref[li:li + 1, :] = jnp.maximum(acc - sm[0:1], 0.0)


def _lower_bounds(hg_gamma):
    return pl.pallas_call(
        _lower_bound_kernel, out_shape=jax.ShapeDtypeStruct(hg_gamma.shape, F32),
        name="hgrn_lower_bounds")(hg_gamma.astype(F32))


def _ln_kernel(x_ref, g_ref, b_ref, o_ref):
    o_ref[...] = _layer_norm_rows(x_ref[...], g_ref[...], b_ref[...])


def _input_layer_norm(x, g, b, tm):
    n, d = x.shape
    return pl.pallas_call(
        _ln_kernel, grid=(n // tm,),
        in_specs=[pl.BlockSpec((tm, d), lambda i: (i, 0)),
                  pl.BlockSpec((1, d), lambda i: (0, 0)),
                  pl.BlockSpec((1, d), lambda i: (0, 0))],
        out_specs=pl.BlockSpec((tm, d), lambda i: (i, 0)),
        out_shape=jax.ShapeDtypeStruct((n, d), F32),
        compiler_params=_cparams(("parallel",)), name="input_ln")(x, g.reshape(1, d), b.reshape(1, d))


def _inproj_kernel(h_ref, w_ref, cos_ref, sa_ref, sb_ref,
                   g_ref, hh_ref, misc_ref, q_ref, qi_ref, k_ref, v_ref, ki_ref,
                   kb_ref, vb_ref, kid_ref):
    hb = h_ref[...].astype(BF16)
    cos, sa, sb = cos_ref[...], sa_ref[...], sb_ref[...]

    def proj(lo, width):
        return _dot(hb, w_ref[:, lo:lo + width])

    def rope(x):
        return x * cos + pltpu.roll(x, LANES - 8, 1) * sa + pltpu.roll(x, 8, 1) * sb

    g_ref[...] = proj(0, 1024)
    hh_ref[...] = proj(1024, 1024)
    for s in range(4):
        q_ref[:, s * LANES:(s + 1) * LANES] = (rope(proj(2048 + s * LANES, LANES)) * 0.125).astype(BF16)
        qi_ref[:, s * LANES:(s + 1) * LANES] = (rope(proj(3072 + s * LANES, LANES)) * 0.125).astype(BF16)
    for s in range(2):
        kr = rope(proj(2560 + s * LANES, LANES))
        k_ref[:, s * LANES:(s + 1) * LANES] = kr
        kb_ref[:, s * LANES:(s + 1) * LANES] = kr.astype(BF16)
    vv = proj(2816, 256)
    v_ref[...] = vv
    vb_ref[...] = vv.astype(BF16)
    misc = rope(proj(3584, LANES))
    misc_ref[...] = misc
    ki_ref[...] = misc[:, 0:HEAD_DIM]
    lane = lax.broadcasted_iota(I32, misc.shape, 1)
    ki_lo = jnp.where(lane < HEAD_DIM, misc, 0.0)
    kid_ref[...] = (ki_lo + pltpu.roll(ki_lo, HEAD_DIM, 1)).astype(BF16)


def _in_projection(h, w_r, cos, sa, sb, tm):
    n = h.shape[0]
    t_blocks = cos.shape[0] // tm
    row = lambda w: pl.BlockSpec((tm, w), lambda i: (i, 0))
    tab = pl.BlockSpec((tm, LANES), lambda i: (i % t_blocks, 0))
    widths = [(1024, F32), (1024, F32), (LANES, F32), (512, BF16), (512, BF16), (256, F32), (256, F32),
              (HEAD_DIM, F32), (256, BF16), (256, BF16), (LANES, BF16)]
    return pl.pallas_call(
        _inproj_kernel, grid=(n // tm,),
        in_specs=[row(D_MODEL), pl.BlockSpec((D_MODEL, PROJ_W), lambda i: (0, 0)), tab, tab, tab],
        out_specs=[row(w) for w, _ in widths],
        out_shape=[jax.ShapeDtypeStruct((n, w), dt) for w, dt in widths],
        compiler_params=_cparams(("parallel",)), name="in_projection")(h, w_r, cos, sa, sb)


def _cumsum_rows(x):
    n = x.shape[0]
    row = lax.broadcasted_iota(I32, x.shape, 0)
    sh = 1
    while sh < n:
        x = x + jnp.where(row >= sh, pltpu.roll(x, sh, 0), 0.0)
        sh *= 2
    return x


def _gla_kernel(g_ref, h_ref, misc_ref, wg_ref, bg_ref, lb_ref, gn_ref, hn_ref, s0_ref,
                o_ref, sT_ref, state, *, n_chunks, t_valid):
    c_idx = pl.program_id(1)

    @pl.when(c_idx == 0)
    def _():
        state[...] = s0_ref[0]

    ch = GLA_CHUNK
    n_sub = ch // GLA_SUB
    row = lax.broadcasted_iota(I32, (ch, ch), 0)
    col = lax.broadcasted_iota(I32, (ch, ch), 1)
    bd = jnp.where((row < HEAD_DIM) == (col < HEAD_DIM), 1.0, 0.0).astype(F32)
    lane = lax.broadcasted_iota(I32, (ch, LANES), 1)
    lane32 = lax.broadcasted_iota(I32, (2 * GLA_SUB, LANES), 1)
    row32 = lax.broadcasted_iota(I32, (2 * GLA_SUB, LANES), 0)
    head_sel = jnp.where((row32 < GLA_SUB) == (lane32 < HEAD_DIM), 1.0, 0.0).astype(F32)
    lane16 = lax.broadcasted_iota(I32, (GLA_SUB, LANES), 1)

    def chunk(ci, carry):
        r0 = pl.multiple_of(ci * ch, ch)
        rows = pl.ds(r0, ch)
        gx = g_ref[rows, :]
        hx = h_ref[rows, :]
        z = _dot(misc_ref[rows, :], wg_ref[...], precision=lax.Precision.HIGHEST) + bg_ref[...]
        lg_gla = _log_sigmoid(z) * (1.0 / GLA_GATE_TAU)
        lb = lb_ref[...]
        zf = hx[:, 256:512]
        a_ = jnp.log(lb)
        c_ = jnp.log(1.0 - lb) + _log_sigmoid(zf)
        lg_hg = jnp.maximum(a_, c_) + jnp.log(1.0 + jnp.exp(-jnp.abs(a_ - c_)))
        k_hg = (1.0 - lb) * _sigmoid(-zf)
        k_gla = gx[:, 256:512]
        if t_valid is not None:
            tpos = c_idx * (n_chunks * ch) + r0 + lax.broadcasted_iota(I32, (ch, 1), 0)
            live = tpos < t_valid
            lg_gla, lg_hg = jnp.where(live, lg_gla, 0.0), jnp.where(live, lg_hg, 0.0)
            k_gla, k_hg = jnp.where(live, k_gla, 0.0), jnp.where(live, k_hg, 0.0)
        hq = hx[:, 0:256]
        gr = gx[:, 768:1024]
        mix = ((gx[:, 0:256] * (HEAD_DIM ** -0.5), k_gla, gx[:, 512:768], _cumsum_rows(lg_gla),
                gr * _sigmoid(gr), gn_ref[...]),
               (hq * _sigmoid(hq), k_hg, hx[:, 512:768], _cumsum_rows(lg_hg),
                _sigmoid(hx[:, 768:1024]), hn_ref[...]))
        pairs = [(mi, p) for mi in range(2) for p in range(2)]
        sl = lambda p: slice(p * LANES, (p + 1) * LANES)

        prep = []
        for mi, p in pairs:
            qm, km, vm, bm = (mix[mi][j][:, sl(p)] for j in range(4))
            b_last = bm[ch - 1:ch, :]
            lhs, kss = [], []
            for i in range(n_sub):
                lo, n = i * GLA_SUB, (i + 1) * GLA_SUB
                ref = bm[lo - 1:lo, :] if i > 0 else jnp.zeros((1, LANES), F32)
                qs = qm[lo:n, :] * jnp.exp(bm[lo:n, :] - ref)
                lhs.append((jnp.concatenate([qs, qs], axis=0) * head_sel).astype(BF16))
                kss.append((km[0:n, :] * jnp.exp(jnp.minimum(ref - bm[0:n, :], GLA_EXP_CLAMP))).astype(BF16))
            prep.append(dict(qd=(qm * jnp.exp(bm)).astype(BF16), vb=vm.astype(BF16), vt=vm.T.astype(BF16),
                             kd=(km * jnp.exp(b_last - bm)).astype(BF16), decay=jnp.exp(b_last),
                             lhs=lhs, kss=kss))

        outs = []
        for idx, (mi, p) in enumerate(pairs):
            pr = prep[idx]
            sT = state[2 * mi + p]
            o_inter = _dot_nt(pr['qd'], sT.astype(BF16))
            scores = [_dot_nt(pr['lhs'][i], pr['kss'][i]) for i in range(n_sub)]
            upd = _dot(pr['vt'], pr['kd'])
            state[2 * mi + p] = (sT * pr['decay'] + upd) * bd
            outs.append((o_inter, scores))

        for idx, (mi, p) in enumerate(pairs):
            o_inter, scores = outs[idx]
            vb = prep[idx]['vb']
            parts = []
            for i in range(n_sub):
                lo, n = i * GLA_SUB, (i + 1) * GLA_SUB
                tt = lo + (lax.broadcasted_iota(I32, (2 * GLA_SUB, n), 0) % GLA_SUB)
                ss = lax.broadcasted_iota(I32, (2 * GLA_SUB, n), 1)
                a = jnp.where(ss <= tt, scores[i], 0.0).astype(BF16)
                oi = _dot(a, vb[0:n, :])
                parts.append(jnp.where(lane16 < HEAD_DIM, oi[0:GLA_SUB, :], oi[GLA_SUB:, :]))
            o = o_inter + jnp.concatenate(parts, axis=0)
            oo = o * o
            s_lo = jnp.sum(jnp.where(lane < HEAD_DIM, oo, 0.0), axis=1, keepdims=True)
            s_hi = jnp.sum(jnp.where(lane < HEAD_DIM, 0.0, oo), axis=1, keepdims=True)
            ms = jnp.where(lane < HEAD_DIM, s_lo, s_hi) * (1.0 / HEAD_DIM)
            on = o * lax.rsqrt(ms + LN_EPS) * mix[mi][5] * mix[mi][4][:, sl(p)]
            o_ref[rows, mi * 256 + p * LANES:mi * 256 + (p + 1) * LANES] = on
        return carry

    lax.fori_loop(0, n_chunks, chunk, 0)

    @pl.when(c_idx == pl.num_programs(1) - 1)
    def _():
        sT_ref[0] = state[...]


def _gla_hgrn(g_all, h_all, misc, wg_pad, bg, lb, gn, hn, s0_bd, batch, t_pad, t_valid):
    ct = min(2 * GLA_CHUNK, t_pad)
    steps = t_pad // ct
    n = batch * t_pad
    row = lambda w: pl.BlockSpec((ct, w), lambda b, c: (b * steps + c, 0))
    const = lambda shp: pl.BlockSpec(shp, lambda b, c: tuple(0 for _ in shp))
    st = pl.BlockSpec((1, 4, LANES, LANES), lambda b, c: (b, 0, 0, 0))
    kern = functools.partial(_gla_kernel, n_chunks=ct // GLA_CHUNK,
                             t_valid=None if t_valid == t_pad else t_valid)
    return pl.pallas_call(
        kern, grid=(batch, steps),
        in_specs=[row(1024), row(1024), row(LANES), const((LANES, 256)), const((1, 256)), const((1, 256)),
                  const((1, LANES)), const((1, LANES)), st],
        out_specs=[row(512), st],
        out_shape=[jax.ShapeDtypeStruct((n, 512), F32),
                   jax.ShapeDtypeStruct((batch, 4, LANES, LANES), F32)],
        scratch_shapes=[pltpu.VMEM((4, LANES, LANES), F32)],
        compiler_params=_cparams(("parallel", "arbitrary")), name="gla_hgrn")(
            g_all, h_all, misc, wg_pad, bg, lb, gn, hn, s0_bd)


def _sort_key(x):
    bits = lax.bitcast_convert_type(x, I32)
    return jnp.where(bits < 0, bits ^ jnp.int32(0x7FFFFFFF), bits)


def _kth_largest_key(count_ge, shape, k):
    int_min = jnp.int32(-2 ** 31)

    def body(it, cand):
        trial = cand | jnp.left_shift(jnp.int32(1), 31 - it)
        cnt = count_ge(trial ^ int_min)
        return jnp.where(cnt >= k, trial, cand)

    cand = lax.fori_loop(0, 32, body, jnp.zeros(shape, I32))
    return cand ^ int_min


def _split_sort_key(x, hi_ref, mid_ref, lo_ref):
    bits = lax.bitcast_convert_type(x + 0.0, I32)
    hi_ref[...] = lax.bitcast_convert_type(bits & jnp.int32(-65536), F32).astype(BF16)
    low = bits & 0xFFFF
    low = jnp.where(bits < 0, 0xFFFF - low, low)
    mid_ref[...] = (low >> 8).astype(F32).astype(BF16)
    lo_ref[...] = (low & 0xFF).astype(F32).astype(BF16)


def _hi_digit_value(u):
    u = jnp.maximum(u, 0x007F)
    pat = jnp.where(u >= 0x8000, u ^ 0x8000, u ^ 0xFFFF)
    return lax.bitcast_convert_type(jnp.left_shift(pat, 16), F32)


def _topk_additive_mask(hi_ref, mid_ref, lo_ref, neg_ref, w, topk, row_pos):
    qb = hi_ref.shape[0]
    n_tiles = w // LANES
    one, zero = jnp.ones((), BF16), jnp.zeros((), BF16)
    tile = lambda ref, c: ref[:, c * LANES:(c + 1) * LANES]
    bcast = lambda col: jnp.broadcast_to(col, (qb, LANES)).astype(BF16)

    def count(ref, pred):
        acc = None
        for c in range(n_tiles):
            m = jnp.where(pred(tile(ref, c)), one, zero)
            acc = m if acc is None else acc + m
        return jnp.sum(acc.astype(F32), axis=1, keepdims=True)

    def search(ref, n_bits, need, value_of):
        def body(it, cand):
            sh = n_bits - 2 - 2 * it
            tvs = [bcast(value_of(cand | jnp.left_shift(jnp.int32(d), sh))) for d in (1, 2, 3)]
            accs = [None, None, None]
            for c in range(n_tiles):
                t = tile(ref, c)
                for j in range(3):
                    m = jnp.where(t >= tvs[j], one, zero)
                    accs[j] = m if accs[j] is None else accs[j] + m
            digit = jnp.zeros((qb, 1), I32)
            for a in accs:
                digit = digit + jnp.where(jnp.sum(a.astype(F32), axis=1, keepdims=True) >= need, 1, 0)
            return cand | jnp.left_shift(digit, sh)

        return bcast(value_of(lax.fori_loop(0, n_bits // 2, body, jnp.zeros((qb, 1), I32))))

    k = jnp.full((qb, 1), float(topk), F32)
    thr_hi = search(hi_ref, 16, k, _hi_digit_value)
    need_mid = k - count(hi_ref, lambda t: t > thr_hi)
    for c in range(n_tiles):
        mid_ref[:, c * LANES:(c + 1) * LANES] = jnp.where(tile(hi_ref, c) == thr_hi, tile(mid_ref, c), -one)
    digit = lambda u: u.astype(F32)
    thr_mid = search(mid_ref, 8, need_mid, digit)
    need_lo = need_mid - count(mid_ref, lambda t: t > thr_mid)
    for c in range(n_tiles):
        lo_ref[:, c * LANES:(c + 1) * LANES] = jnp.where(tile(mid_ref, c) == thr_mid, tile(lo_ref, c), -one)
    thr_lo = search(lo_ref, 8, need_lo, digit)
    need_eq = need_lo - count(lo_ref, lambda t: t > thr_lo)
    n_eq = count(lo_ref, lambda t: t == thr_lo)

    def write(keep_eq):
        for c in range(n_tiles):
            keep = jnp.where((tile(hi_ref, c) > thr_hi) | (tile(mid_ref, c) > thr_mid)
                             | (tile(lo_ref, c) > thr_lo), one, zero).astype(F32) + keep_eq(c)
            col_pos = c * LANES + lax.broadcasted_iota(I32, (1, LANES), 1)
            neg_ref[:, c * LANES:(c + 1) * LANES] = jnp.where((keep > 0.5) & (col_pos <= row_pos), 0.0, NEG_BIG)

    eq_tile = lambda c: jnp.where(tile(lo_ref, c) == thr_lo, one, zero)
    write(lambda c: eq_tile(c).astype(F32))

    finite = thr_hi[:, 0:1].astype(F32) > -jnp.inf
    excess = jnp.max(jnp.where(finite, n_eq - need_eq, 0.0))

    @pl.when(excess > 0.5)
    def _():
        r = lax.broadcasted_iota(I32, (LANES, LANES), 0)
        cc = lax.broadcasted_iota(I32, (LANES, LANES), 1)
        upper = jnp.where(r <= cc, 1.0, 0.0).astype(BF16)
        seen = [jnp.zeros((qb, 1), F32)]
        kept = []
        for c in range(n_tiles):
            eq = eq_tile(c)
            prefix = _dot(eq, upper) + seen[-1]
            kept.append(jnp.where(prefix <= need_eq, eq.astype(F32), 0.0))
            seen.append(prefix[:, LANES - 1:LANES])
        write(lambda c: kept[c])


def _dsa_prompt_kernel(q_ref, qi_ref, misc_ref, kb_ref, vb_ref, kid_ref, o_ref,
                       sc_ref, hi_ref, mid_ref, lo_ref, neg_ref, *, seq, kt, topk):
    qb = q_ref.shape[0]
    i = pl.program_id(1)
    q_lo = i * qb
    n_t = seq // kt
    n_act = (q_lo + qb - 1) // kt + 1
    row_pos = q_lo + lax.broadcasted_iota(I32, (qb, 1), 0)
    lane_q = lax.broadcasted_iota(I32, (qb, LANES), 1)
    half = (lane_q < HEAD_DIM, lane_q >= HEAD_DIM)
    w8 = misc_ref[...][:, MISC_IW:MISC_IW + IDX_HEADS] * (IDX_HEADS ** -0.5)

    sc_ref[...] = jnp.full(sc_ref.shape, -jnp.inf, F32)
    for t in range(n_t):
        @pl.when(t < n_act)
        def _(t=t):
            cols = slice(t * kt, (t + 1) * kt)
            kid = kid_ref[cols, :]
            acc = jnp.zeros((qb, kt), F32)
            for j in range(IDX_HEADS // 2):
                slab = qi_ref[:, j * LANES:(j + 1) * LANES]
                for c in range(2):
                    lhs = jnp.where(half[c], slab, jnp.zeros_like(slab))
                    logit = _dot_nt(lhs, kid)
                    acc = acc + jnp.maximum(logit, 0.0) * w8[:, 2 * j + c:2 * j + c + 1]
            col_pos = t * kt + lax.broadcasted_iota(I32, (1, kt), 1)
            sc_ref[:, cols] = jnp.where(col_pos <= row_pos, acc, -jnp.inf)
    _split_sort_key(sc_ref[...], hi_ref, mid_ref, lo_ref)

    for nt in range(1, n_t + 1):
        @pl.when(n_act == nt)
        def _(nt=nt):
            w = nt * kt
            _topk_additive_mask(hi_ref, mid_ref, lo_ref, neg_ref, w, topk, row_pos)
            for p in range(2):
                kp = kb_ref[0:w, p * LANES:(p + 1) * LANES]
                vp = vb_ref[0:w, p * LANES:(p + 1) * LANES]
                for g in range(2):
                    s = 2 * p + g
                    qs = q_ref[:, s * LANES:(s + 1) * LANES]
                    outs = []
                    for c in range(2):
                        lhs = jnp.where(half[c], qs, jnp.zeros_like(qs))
                        sc = _dot_nt(lhs, kp) + neg_ref[:, 0:w]
                        m = jnp.max(sc, axis=1, keepdims=True)
                        pe = jnp.exp(sc - m)
                        l = jnp.sum(pe, axis=1, keepdims=True)
                        outs.append(_dot(pe.astype(BF16), vp) / l)
                    o_ref[:, s * LANES:(s + 1) * LANES] = jnp.where(half[0], outs[0], outs[1])


def _dsa_prompt(q, qi, misc, kb, vb, kid, batch, seq, qb=128):
    n = batch * seq
    nq = seq // qb
    kt = min(512, seq)
    topk = min(TOPK_MAX, seq // 4)
    rowq = lambda w: pl.BlockSpec((qb, w), lambda b, i: (b * nq + i, 0))
    full = lambda w: pl.BlockSpec((seq, w), lambda b, i: (b, 0))
    kern = functools.partial(_dsa_prompt_kernel, seq=seq, kt=kt, topk=topk)
    return pl.pallas_call(
        kern, grid=(batch, nq),
        in_specs=[rowq(512), rowq(512), rowq(LANES), full(256), full(256), full(LANES)],
        out_specs=rowq(512),
        out_shape=jax.ShapeDtypeStruct((n, 512), F32),
        scratch_shapes=[pltpu.VMEM((qb, seq), F32), pltpu.VMEM((qb, seq), BF16), pltpu.VMEM((qb, seq), BF16),
                        pltpu.VMEM((qb, seq), BF16), pltpu.VMEM((qb, seq), F32)],
        compiler_params=_cparams(("parallel", "arbitrary")), name="dsa_prompt")(q, qi, misc, kb, vb, kid)


PAGES_PER_STEP = 8
DSA_Q_HEADS = 8
DSA_KV_HEADS = 4


def _dsa_sample_score_kernel(pt_ref, qi_ref, w_ref, *refs):
    pages, out_ref = refs[:PAGES_PER_STEP], refs[PAGES_PER_STEP]
    qi = qi_ref[0]
    w = w_ref[0] * (IDX_HEADS ** -0.5)
    for j, pg in enumerate(pages):
        logit = _dot(qi, pg[...].astype(BF16))
        out_ref[0, j:j + 1, :] = jnp.sum(jnp.maximum(logit, 0.0) * w, axis=0, keepdims=True)


def _dsa_sample_scores(page_table, cache_kidx_t, li, qi3, w3):
    bd, n_pages = page_table.shape
    groups = n_pages // PAGES_PER_STEP

    def page_spec(j):
        return pl.BlockSpec((None, None, HEAD_DIM, PAGE_SIZE),
                            lambda b, g, pt: (li, pt[b, g * PAGES_PER_STEP + j], 0, 0))

    grid_spec = pltpu.PrefetchScalarGridSpec(
        num_scalar_prefetch=1, grid=(bd, groups),
        in_specs=[pl.BlockSpec((1, IDX_HEADS, HEAD_DIM), lambda b, g, pt: (b, 0, 0)),
                  pl.BlockSpec((1, IDX_HEADS, 1), lambda b, g, pt: (b, 0, 0))]
        + [page_spec(j) for j in range(PAGES_PER_STEP)],
        out_specs=pl.BlockSpec((1, PAGES_PER_STEP, PAGE_SIZE), lambda b, g, pt: (b, g, 0)))
    return pl.pallas_call(
        _dsa_sample_score_kernel, grid_spec=grid_spec,
        out_shape=jax.ShapeDtypeStruct((bd, n_pages, PAGE_SIZE), F32),
        compiler_params=_cparams(("parallel", "arbitrary")), name="dsa_sample_scores")(
            page_table, qi3, w3, *([cache_kidx_t] * PAGES_PER_STEP))


def _dsa_sample_threshold_kernel(sc_ref, qi_ref, ki_ref, w_ref, thr_ref, self_ref, key_ref, *, topk):
    qi = qi_ref[...].astype(F32)
    ki = ki_ref[...].astype(BF16).astype(F32)
    w = w_ref[...] * (IDX_HEADS ** -0.5)
    logit = jnp.sum(qi * ki, axis=2, keepdims=True)
    self_sc = jnp.sum(jnp.maximum(logit, 0.0) * w, axis=1)
    self_key = _sort_key(self_sc)
    key_ref[...] = _sort_key(sc_ref[...])

    def count_ge(thr):
        c = jnp.sum(jnp.where(key_ref[...] >= thr, 1.0, 0.0), axis=1, keepdims=True)
        return c + jnp.where(self_key >= thr, 1.0, 0.0)

    thr = _kth_largest_key(count_ge, self_key.shape, float(topk))
    thr_ref[...] = thr
    self_ref[...] = jnp.where(self_key >= thr, 1.0, 0.0)


def _dsa_sample_threshold(scores2, qi3, ki3, w3, topk):
    bd, past = scores2.shape
    kern = functools.partial(_dsa_sample_threshold_kernel, topk=topk)
    return pl.pallas_call(
        kern,
        out_shape=[jax.ShapeDtypeStruct((bd, 1), I32), jax.ShapeDtypeStruct((bd, 1), F32)],
        scratch_shapes=[pltpu.VMEM((bd, past), I32)],
        compiler_params=pltpu.CompilerParams(vmem_limit_bytes=VMEM_LIMIT),
        name="dsa_sample_threshold")(scores2, qi3, ki3, w3)


def _dsa_sample_attn_kernel(pt_ref, qbd_ref, sc_ref, thr_ref, self_ref, kn_ref, vn_ref, *refs,
                            n_groups):
    kpages = refs[:PAGES_PER_STEP]
    vpages = refs[PAGES_PER_STEP:2 * PAGES_PER_STEP]
    o_ref, m_ref, l_ref, acc_ref = refs[2 * PAGES_PER_STEP:]
    g = pl.program_id(1)
    nh = qbd_ref.shape[1]

    @pl.when(g == 0)
    def _():
        m_ref[...] = jnp.full(m_ref.shape, NEG_BIG, F32)
        l_ref[...] = jnp.zeros(l_ref.shape, F32)
        acc_ref[...] = jnp.zeros(acc_ref.shape, F32)

    qbd = qbd_ref[0]
    thr = thr_ref[0]

    for j in range(PAGES_PER_STEP):
        keep = _sort_key(sc_ref[0, j:j + 1, :]) >= thr
        sc = jnp.where(keep, _dot(qbd, kpages[j][...].astype(BF16)), NEG_BIG)
        m_old = m_ref[...]
        m_new = jnp.maximum(m_old, jnp.max(sc, axis=1, keepdims=True))
        alpha = jnp.exp(m_old - m_new)
        pe = jnp.where(keep, jnp.exp(sc - m_new), 0.0)
        l_ref[...] = alpha * l_ref[...] + jnp.sum(pe, axis=1, keepdims=True)
        acc_ref[...] = alpha * acc_ref[...] + _dot_nt(pe.astype(BF16), vpages[j][...].astype(BF16))
        m_ref[...] = m_new

    @pl.when(g == n_groups - 1)
    def _():
        kn = kn_ref[0].astype(BF16).astype(F32)
        sc = jnp.sum(qbd.astype(F32) * kn, axis=1, keepdims=True)
        sc = jnp.where(self_ref[0] > 0.5, sc, NEG_BIG)
        m_old = m_ref[...]
        m_new = jnp.maximum(m_old, sc)
        alpha = jnp.exp(m_old - m_new)
        pe = jnp.exp(sc - m_new)
        pe = jnp.where(sc > 0.5 * NEG_BIG, pe, 0.0)
        l = alpha * l_ref[...] + pe
        acc = alpha * acc_ref[...] + pe * vn_ref[0].astype(BF16).astype(F32)
        res = acc / l
        kvh = lax.broadcasted_iota(I32, (nh, HEAD_DIM), 0) // 2
        out = jnp.zeros((nh, HEAD_DIM), F32)
        for c in range(4):
            out = out + jnp.where(kvh == c, res[:, c * HEAD_DIM:(c + 1) * HEAD_DIM], 0.0)
        o_ref[0] = out


def _dsa_sample_attention(page_table, cache_k2, cache_v2, li, qbd, scores3, thr3, self3, kn3, vn3):
    bd, n_pages = page_table.shape
    groups = n_pages // PAGES_PER_STEP
    kvw = cache_k2.shape[-2]

    def page_spec(j):
        return pl.BlockSpec((None, None, kvw, PAGE_SIZE),
                            lambda b, g, pt: (li, pt[b, g * PAGES_PER_STEP + j], 0, 0))

    per_b = lambda shp: pl.BlockSpec((1,) + shp, lambda b, g, pt: (b, 0, 0))
    grid_spec = pltpu.PrefetchScalarGridSpec(
        num_scalar_prefetch=1, grid=(bd, groups),
        in_specs=[per_b((DSA_Q_HEADS, kvw)),
                  pl.BlockSpec((1, PAGES_PER_STEP, PAGE_SIZE), lambda b, g, pt: (b, g, 0)),
                  per_b((1, 1)), per_b((1, 1)), per_b((1, kvw)), per_b((1, kvw))]
        + [page_spec(j) for j in range(PAGES_PER_STEP)] * 2,
        out_specs=per_b((DSA_Q_HEADS, HEAD_DIM)),
        scratch_shapes=[pltpu.VMEM((DSA_Q_HEADS, 1), F32), pltpu.VMEM((DSA_Q_HEADS, 1), F32),
                        pltpu.VMEM((DSA_Q_HEADS, kvw), F32)])
    kern = functools.partial(_dsa_sample_attn_kernel, n_groups=groups)
    return pl.pallas_call(
        kern, grid_spec=grid_spec,
        out_shape=jax.ShapeDtypeStruct((bd, DSA_Q_HEADS, HEAD_DIM), F32),
        compiler_params=_cparams(("parallel", "arbitrary")), name="dsa_sample_attention")(
            page_table, qbd, scores3, thr3, self3, kn3, vn3,
            *([cache_k2] * PAGES_PER_STEP), *([cache_v2] * PAGES_PER_STEP))


def _post_mix_kernel(oa_ref, od_ref, x_ref, p_ref, woa_ref, wod_ref, g1_ref, b1_ref, wr_ref, br_ref,
                     wpg_ref, wpp_ref, cnt0_ref, x1t_ref, base_ref, route_ref, cnt_ref):
    tb = x_ref.shape[0]

    @pl.when(pl.program_id(0) == 0)
    def _():
        cnt_ref[...] = cnt0_ref[...]

    mixv = _dot(oa_ref[...].astype(BF16), woa_ref[...]) + _dot(od_ref[...].astype(BF16), wod_ref[...])
    x1 = _layer_norm_rows(DEEPNORM_ALPHA * x_ref[...] + mixv, g1_ref[...], b1_ref[...])
    x1b = x1.astype(BF16)
    for j in range(SUBLANES):
        x1t_ref[pl.ds(j, tb, stride=SUBLANES), :] = x1[:, j * LANES:(j + 1) * LANES]
    ple = _sigmoid(_dot(x1b, wpg_ref[...])) * _dot(p_ref[...].astype(BF16), wpp_ref[...])
    base_ref[...] = DEEPNORM_ALPHA * x1 + ple

    logits = _dot(x1, wr_ref[...], precision=lax.Precision.HIGHEST) + br_ref[...]
    lane = lax.broadcasted_iota(I32, (tb, LANES), 1)
    lanef = lane.astype(F32)
    vals, hots = [], []
    work = logits
    for _ in range(TOP_K):
        m = jnp.max(work, axis=1, keepdims=True)
        idx = jnp.min(jnp.where(work == m, lanef, float(LANES)), axis=1, keepdims=True)
        hot = lanef == idx
        vals.append(m)
        hots.append(hot)
        work = jnp.where(hot, -jnp.inf, work)
    exps = [jnp.exp(v - vals[0]) for v in vals]
    den = exps[0] + exps[1] + exps[2] + exps[3]
    oh = jnp.zeros((tb, LANES), F32)
    for hot in hots:
        oh = oh + jnp.where(hot, 1.0, 0.0)
    r = lax.broadcasted_iota(I32, (tb, tb), 0)
    c = lax.broadcasted_iota(I32, (tb, tb), 1)
    strict = jnp.where(c < r, 1.0, 0.0).astype(BF16)
    before = _dot(strict, oh.astype(BF16)) + cnt_ref[...]
    route = jnp.zeros((tb, LANES), F32)
    for k in range(TOP_K):
        gate = exps[k] / den
        eidx = jnp.sum(jnp.where(hots[k], lanef, 0.0), axis=1, keepdims=True)
        rank = jnp.sum(jnp.where(hots[k], before, 0.0), axis=1, keepdims=True)
        route = route + jnp.where(lane == k, gate, 0.0) + jnp.where(lane == TOP_K + k, eidx, 0.0) \
            + jnp.where(lane == 2 * TOP_K + k, rank, 0.0)
    route_ref[...] = route
    cnt_ref[...] = cnt_ref[...] + jnp.sum(oh, axis=0, keepdims=True)


def _post_mix(oa, od, x, p, woa, wod, g1, b1, wr, br, wpg, wpp, cnt0, tb):
    n = x.shape[0]
    row = lambda w: pl.BlockSpec((tb, w), lambda i: (i, 0))
    const = lambda a: pl.BlockSpec(a.shape, lambda i: tuple(0 for _ in a.shape))
    return pl.pallas_call(
        _post_mix_kernel, grid=(n // tb,),
        in_specs=[row(512), row(512), row(D_MODEL), row(p.shape[1]), const(woa), const(wod), const(g1),
                  const(b1), const(wr), const(br), const(wpg), const(wpp), const(cnt0)],
        out_specs=[pl.BlockSpec((tb * SUBLANES, LANES), lambda i: (i, 0)), row(D_MODEL), row(LANES),
                   pl.BlockSpec((1, LANES), lambda i: (0, 0))],
        out_shape=[jax.ShapeDtypeStruct((n * SUBLANES, LANES), F32), jax.ShapeDtypeStruct((n, D_MODEL), F32),
                   jax.ShapeDtypeStruct((n, LANES), F32), jax.ShapeDtypeStruct((1, LANES), F32)],
        compiler_params=_cparams(("arbitrary",)), name="post_mix_router")(
            oa, od, x, p, woa, wod, g1, b1, wr, br, wpg, wpp, cnt0)


def _row_copy(src, src_row, dst, dst_row, sem):
    return pltpu.make_async_copy(src.at[pl.ds(src_row * SUBLANES, SUBLANES), :],
                                 dst.at[pl.ds(dst_row * SUBLANES, SUBLANES), :], sem)


def _moe_scatter_kernel(dest_ref, x_ref, xs_in_ref, xs_ref, sem):
    del xs_in_ref
    n_copies = dest_ref.shape[-1]

    def issue(c, carry):
        for u in range(DMA_UNROLL):
            a = c * DMA_UNROLL + u
            _row_copy(x_ref, a // TOP_K, xs_ref, dest_ref[0, 0, a], sem).start()
        return carry

    lax.fori_loop(0, n_copies // DMA_UNROLL, issue, 0)
    _drain_row_copies(x_ref, xs_ref, sem, n_copies)


def _drain_row_copies(src, dst, sem, n_copies):
    def drain(c, carry):
        for _ in range(DMA_WAIT_UNROLL):
            _row_copy(src, 0, dst, 0, sem).wait()
        return carry

    lax.fori_loop(0, n_copies // DMA_WAIT_UNROLL, drain, 0)
    for _ in range(n_copies % DMA_WAIT_UNROLL):
        _row_copy(src, 0, dst, 0, sem).wait()


def _moe_scatter(dest3, x1t, n_slots, tb):
    n = x1t.shape[0] // SUBLANES
    zeros = jnp.zeros((n_slots * SUBLANES, LANES), F32)
    return pl.pallas_call(
        _moe_scatter_kernel, grid=(n // tb,),
        in_specs=[pl.BlockSpec((1, 1, tb * TOP_K), lambda i: (i, 0, 0), memory_space=pltpu.SMEM),
                  pl.BlockSpec((tb * SUBLANES, LANES), lambda i: (i, 0)),
                  pl.BlockSpec(memory_space=pl.ANY)],
        out_specs=pl.BlockSpec(memory_space=pl.ANY),
        out_shape=jax.ShapeDtypeStruct(zeros.shape, F32),
        scratch_shapes=[pltpu.SemaphoreType.DMA],
        input_output_aliases={2: 0},
        compiler_params=_cparams(("arbitrary",)), name="moe_scatter")(dest3, x1t, zeros)


def _moe_expert_kernel(be_ref, nu_ref, xs_ref, wg_ref, bg_ref, wu_ref, bu_ref, wd_ref, bd_ref, ys_ref,
                       wgb, wub, wdb):
    i = pl.program_id(0)
    blk = xs_ref.shape[0] // SUBLANES

    @pl.when(i < nu_ref[0])
    def _():
        prev = be_ref[jnp.maximum(i - 1, 0)]

        @pl.when((i == 0) | (prev != be_ref[i]))
        def _():
            wgb[...] = wg_ref[...].astype(BF16)
            wub[...] = wu_ref[...].astype(BF16)
            wdb[...] = wd_ref[...].astype(BF16)

        x = jnp.concatenate([xs_ref[pl.ds(j, blk, stride=SUBLANES), :] for j in range(SUBLANES)],
                            axis=1).astype(BF16)
        g = jnp.minimum(_dot(x, wgb[...]) + bg_ref[...], SWIGLU_LIMIT)
        u = jnp.clip(_dot(x, wub[...]) + bu_ref[...], -SWIGLU_LIMIT, SWIGLU_LIMIT)
        hdn = (u + 1.0) * g * _sigmoid(SWIGLU_ALPHA * g)
        y = _dot(hdn.astype(BF16), wdb[...]) + bd_ref[...]
        for j in range(SUBLANES):
            ys_ref[pl.ds(j, blk, stride=SUBLANES), :] = y[:, j * LANES:(j + 1) * LANES]

    @pl.when(i >= nu_ref[0])
    def _():
        ys_ref[...] = jnp.zeros(ys_ref.shape, F32)


def _moe_experts(block_expert, n_used, xs, li, wg, bg, wu, bu, wd, bd):
    n_blocks = block_expert.shape[0]
    d, f = wg.shape[2], wg.shape[3]

    def blk_idx(i, be, nu):
        return (jnp.minimum(i, nu[0] - 1), 0)

    def w_idx(i, be, nu):
        return (li, be[jnp.minimum(i, nu[0] - 1)], 0, 0)

    rows = pl.BlockSpec((MOE_BLK * SUBLANES, LANES), blk_idx)
    out_rows = pl.BlockSpec((MOE_BLK * SUBLANES, LANES), lambda i, be, nu: (i, 0))
    grid_spec = pltpu.PrefetchScalarGridSpec(
        num_scalar_prefetch=2, grid=(n_blocks,),
        in_specs=[rows,
                  pl.BlockSpec((None, None, d, f), w_idx), pl.BlockSpec((None, None, 1, f), w_idx),
                  pl.BlockSpec((None, None, d, f), w_idx), pl.BlockSpec((None, None, 1, f), w_idx),
                  pl.BlockSpec((None, None, f, d), w_idx), pl.BlockSpec((None, None, 1, d), w_idx)],
        out_specs=out_rows,
        scratch_shapes=[pltpu.VMEM((d, f), BF16), pltpu.VMEM((d, f), BF16), pltpu.VMEM((f, d), BF16)])
    return pl.pallas_call(
        _moe_expert_kernel, grid_spec=grid_spec,
        out_shape=jax.ShapeDtypeStruct(xs.shape, F32),
        compiler_params=_cparams(("arbitrary",)), name="moe_experts")(
            block_expert, n_used, xs, wg, bg, wu, bu, wd, bd)


def _moe_combine_kernel(dest_ref, ys_ref, route_ref, base_ref, g2_ref, b2_ref, o_ref, yg, sem):
    tb = base_ref.shape[0]
    n_copies = tb * TOP_K

    def issue(c, carry):
        for u in range(DMA_UNROLL):
            a = c * DMA_UNROLL + u
            n, k = c * (DMA_UNROLL // TOP_K) + u // TOP_K, u % TOP_K
            _row_copy(ys_ref, dest_ref[0, 0, a], yg, k * tb + n, sem).start()
        return carry

    lax.fori_loop(0, n_copies // DMA_UNROLL, issue, 0)
    _drain_row_copies(ys_ref, yg, sem, n_copies)

    gates = route_ref[...][:, 0:TOP_K]
    cols = []
    for j in range(SUBLANES):
        acc = jnp.zeros((tb, LANES), F32)
        for k in range(TOP_K):
            acc = acc + gates[:, k:k + 1] * yg[pl.ds(k * tb * SUBLANES + j, tb, stride=SUBLANES), :]
        cols.append(acc)
    ffn = jnp.concatenate(cols, axis=1)
    o_ref[...] = _layer_norm_rows(base_ref[...] + ffn, g2_ref[...], b2_ref[...])


def _moe_combine(dest3, ys, route, base, g2, b2, tb):
    n = base.shape[0]
    row = lambda w: pl.BlockSpec((tb, w), lambda i: (i, 0))
    const = lambda a: pl.BlockSpec(a.shape, lambda i: tuple(0 for _ in a.shape))
    return pl.pallas_call(
        _moe_combine_kernel, grid=(n // tb,),
        in_specs=[pl.BlockSpec((1, 1, tb * TOP_K), lambda i: (i, 0, 0), memory_space=pltpu.SMEM),
                  pl.BlockSpec(memory_space=pl.ANY), row(LANES), row(D_MODEL), const(g2), const(b2)],
        out_specs=row(D_MODEL),
        out_shape=jax.ShapeDtypeStruct((n, D_MODEL), F32),
        scratch_shapes=[pltpu.VMEM((TOP_K * tb * SUBLANES, LANES), F32), pltpu.SemaphoreType.DMA],
        compiler_params=_cparams(("arbitrary",)), name="moe_combine")(dest3, ys, route, base, g2, b2)


def _moe_layer(x1t, route, cnt, base, lw, ew, li, tb):
    n = base.shape[0]
    a = n * TOP_K
    n_blocks = a // MOE_BLK + N_EXPERTS
    counts = cnt[0, :N_EXPERTS].astype(I32)
    padded = (counts + MOE_BLK - 1) // MOE_BLK * MOE_BLK
    pad_end = jnp.cumsum(padded)
    pad_start = pad_end - padded
    top_idx = route[:, TOP_K:2 * TOP_K].astype(I32)
    rank = route[:, 2 * TOP_K:3 * TOP_K].astype(I32)
    dest = pad_start[top_idx] + rank
    dest3 = dest.reshape(n // tb, 1, tb * TOP_K)
    block_start = jnp.arange(n_blocks, dtype=I32) * MOE_BLK
    block_expert = jnp.minimum(jnp.sum((pad_end[None, :] <= block_start[:, None]).astype(I32), axis=1),
                               N_EXPERTS - 1).astype(I32)
    n_used = (pad_end[-1:] // MOE_BLK).astype(I32)
    xs = _moe_scatter(dest3, x1t, n_blocks * MOE_BLK, tb)
    ys = _moe_experts(block_expert, n_used, xs, li, *ew)
    return _moe_combine(dest3, ys, route, base, lw['ln2_g'], lw['ln2_b'], tb)


_DQ_PROMPT_ORDER = (0, 2, 1, 3, 4, 6, 5, 7)


def _relayout_w_in(w, head_order):
    col = lambda name, width: w[:, _OFF[name]:_OFF[name] + width]
    dq = col('dq', 512).reshape(D_MODEL, 8, HEAD_DIM)[:, jnp.array(head_order), :].reshape(D_MODEL, 512)
    z = lambda width: jnp.zeros((D_MODEL, width), w.dtype)
    misc = jnp.concatenate([col('ik', 64), z(16), col('iw', 8), col('glr', 16), z(24)], axis=1)
    out = jnp.concatenate([col('gq', 256), col('gk', 256), col('gv', 256), col('gr', 256),
                           col('hq', 256), col('hf', 256), col('hi', 256), col('hgt', 256),
                           dq, col('dk', 256), col('dv', 256), col('iq', 512), misc], axis=1)
    return out.astype(BF16)


def _rope_tables(pos):
    half = ROPE_DIM // 2
    inv = ROPE_THETA ** (-jnp.arange(half, dtype=F32) / half)
    ang = pos.astype(F32)[:, None] * inv[None, :]
    cos, sin = jnp.cos(ang), jnp.sin(ang)
    t = pos.shape[0]
    one, zero = jnp.ones((t, HEAD_DIM - ROPE_DIM), F32), jnp.zeros((t, HEAD_DIM - ROPE_DIM), F32)
    z8 = jnp.zeros((t, half), F32)
    c64 = jnp.concatenate([cos, cos, one], axis=1)
    a64 = jnp.concatenate([-sin, z8, zero], axis=1)
    b64 = jnp.concatenate([z8, sin, zero], axis=1)
    tile2 = lambda a: jnp.concatenate([a, a], axis=1)
    return tile2(c64), tile2(a64), tile2(b64)


def _layer_weights(li, w):
    wo = w['w_out'][li]
    wod = wo[512:].reshape(8, HEAD_DIM, D_MODEL)
    wg_pad = jnp.zeros((LANES, 256), F32).at[MISC_GLR:MISC_GLR + 16].set(w['w_gla_gate'][li])
    wr = jnp.zeros((D_MODEL, LANES), F32).at[:, :N_EXPERTS].set(w['w_router'][li])
    br = jnp.full((1, LANES), NEG_BIG, F32).at[0, :N_EXPERTS].set(w['b_router'][li])
    tile2 = lambda g: jnp.concatenate([g, g]).reshape(1, LANES)
    return dict(
        w_in_p=_relayout_w_in(w['w_in'][li], _DQ_PROMPT_ORDER),
        wg_pad=wg_pad, bg=w['b_gla_gate'][li].reshape(1, 256),
        gn=tile2(w['gla_norm_g'][li]), hn=tile2(w['hg_norm_g'][li]),
        woa=wo[:512].astype(BF16),
        wod_p=wod[jnp.array(_DQ_PROMPT_ORDER)].reshape(512, D_MODEL).astype(BF16),
        ln1_g=w['ln1_g'][li].reshape(1, D_MODEL), ln1_b=w['ln1_b'][li].reshape(1, D_MODEL),
        wr=wr, br=br,
        wpg=w['w_ple_gate'][li].astype(BF16), wpp=w['w_ple_proj'][li].astype(BF16),
        ln2_g=w['ln2_g'][li].reshape(1, D_MODEL), ln2_b=w['ln2_b'][li].reshape(1, D_MODEL))


def _state_to_blockdiag(s):
    b = s.shape[0]
    st = jnp.swapaxes(s, -1, -2).reshape(b, 2, 2, HEAD_DIM, HEAD_DIM)
    z = jnp.zeros_like(st[:, :, 0])
    top = jnp.concatenate([st[:, :, 0], z], axis=-1)
    bot = jnp.concatenate([z, st[:, :, 1]], axis=-1)
    return jnp.concatenate([top, bot], axis=-2)


def _state_from_blockdiag(sbd):
    b = sbd.shape[0]
    h0 = sbd[:, :, :HEAD_DIM, :HEAD_DIM]
    h1 = sbd[:, :, HEAD_DIM:, HEAD_DIM:]
    st = jnp.stack([h0, h1], axis=2).reshape(b, 4, HEAD_DIM, HEAD_DIM)
    return jnp.swapaxes(st, -1, -2)


def kernel(x_prompt, x_sample, cache_k, cache_v, cache_kidx, state_gla, state_hgrn, page_table, p_prompt, p_sample, ln_in_g, ln_in_b, w_in, w_gla_gate, b_gla_gate, gla_norm_g, hg_gamma, hg_norm_g, w_out, ln1_g, ln1_b, w_router, b_router, w_exp_gate, b_exp_gate, w_exp_up, b_exp_up, w_exp_down, b_exp_down, ln2_g, ln2_b, w_ple_gate, w_ple_proj):
    bp, t, d = x_prompt.shape
    bs, ts, _ = x_sample.shape
    assert ts == 1 and d == D_MODEL
    depth = w_in.shape[0]
    n_pages = page_table.shape[1]
    past = n_pages * PAGE_SIZE
    n_p = bp * t
    weights = dict(w_in=w_in, w_gla_gate=w_gla_gate, b_gla_gate=b_gla_gate, gla_norm_g=gla_norm_g,
                   hg_norm_g=hg_norm_g, w_out=w_out, ln1_g=ln1_g, ln1_b=ln1_b, w_router=w_router,
                   b_router=b_router, w_exp_gate=w_exp_gate, b_exp_gate=b_exp_gate, w_exp_up=w_exp_up,
                   b_exp_up=b_exp_up, w_exp_down=w_exp_down, b_exp_down=b_exp_down, ln2_g=ln2_g,
                   ln2_b=ln2_b, w_ple_gate=w_ple_gate, w_ple_proj=w_ple_proj)

    tm_p = min(512, t)
    tb_p = min(256, t)
    lbs = _lower_bounds(hg_gamma)
    tabs_p = _rope_tables(jnp.arange(t, dtype=I32))
    tabs_s = _rope_tables(jnp.full((bs,), past, I32))
    xp = _input_layer_norm(x_prompt.reshape(n_p, d), ln_in_g, ln_in_b, tm_p)
    xs = _input_layer_norm(x_sample.reshape(bs, d), ln_in_g, ln_in_b, bs)
    pages_t = lambda c: jnp.transpose(c, (0, 1, 3, 4, 2)).reshape(c.shape[0], c.shape[1], -1, PAGE_SIZE)
    cache_k2, cache_v2 = pages_t(cache_k), pages_t(cache_v)
    cache_kidx_t = jnp.transpose(cache_kidx, (0, 1, 3, 2))
    bias4 = lambda b: b.reshape(b.shape[0], b.shape[1], 1, b.shape[2])
    ew = (w_exp_gate, bias4(b_exp_gate), w_exp_up, bias4(b_exp_up), w_exp_down, bias4(b_exp_down))
    head_perm = jnp.array(_DQ_PROMPT_ORDER)
    zero_state = jnp.zeros((bp, 4, LANES, LANES), F32)
    zero_cnt = jnp.zeros((1, LANES), F32)
    topk_s = min(TOPK_MAX, (past + ts) // 4)
    head_of_row = jnp.arange(DSA_Q_HEADS)[:, None] // 2
    kv_of_lane = jnp.arange(DSA_KV_HEADS * HEAD_DIM)[None, :] // HEAD_DIM

    outs = {k: [] for k in ('kp', 'vp', 'kip', 'sgp', 'shp', 'ks', 'vs', 'kis', 'sgs', 'shs')}
    for li in range(depth):
        lw = _layer_weights(li, weights)
        lb = lbs[li].reshape(1, 256)

        (g_all, h_all, misc, q, qi, k, v, ki, kb, vb, kid) = _in_projection(xp, lw['w_in_p'], *tabs_p, tm_p)
        o_lin, st = _gla_hgrn(g_all, h_all, misc, lw['wg_pad'], lw['bg'], lb, lw['gn'], lw['hn'],
                              zero_state, bp, t, t)
        o_dsa = _dsa_prompt(q, qi, misc, kb, vb, kid, bp, t)
        x1t, base, route, cnt = _post_mix(o_lin, o_dsa, xp, p_prompt[li].reshape(n_p, -1), lw['woa'],
                                          lw['wod_p'], lw['ln1_g'], lw['ln1_b'], lw['wr'], lw['br'],
                                          lw['wpg'], lw['wpp'], zero_cnt, tb_p)
        xp = _moe_layer(x1t, route, cnt, base, lw, ew, li, tb_p)
        outs['kp'].append(k.reshape(bp, t, DSA_KV_HEADS, HEAD_DIM))
        outs['vp'].append(v.reshape(bp, t, DSA_KV_HEADS, HEAD_DIM))
        outs['kip'].append(ki.reshape(bp, t, HEAD_DIM))
        outs['sgp'].append(_state_from_blockdiag(st[:, 0:2]))
        outs['shp'].append(_state_from_blockdiag(st[:, 2:4]))

        (g_all, h_all, misc, q, qi, k, v, ki, kb, vb, kid) = _in_projection(xs, lw['w_in_p'], *tabs_s, bs)
        pad = lambda a: jnp.pad(a.reshape(bs, 1, -1), ((0, 0), (0, GLA_CHUNK - 1), (0, 0))).reshape(
            bs * GLA_CHUNK, -1)
        s0 = jnp.concatenate([_state_to_blockdiag(state_gla[li]), _state_to_blockdiag(state_hgrn[li])], axis=1)
        o_lin, st = _gla_hgrn(pad(g_all), pad(h_all), pad(misc), lw['wg_pad'], lw['bg'], lb, lw['gn'],
                              lw['hn'], s0, bs, GLA_CHUNK, 1)
        o_lin = o_lin.reshape(bs, GLA_CHUNK, -1)[:, 0]
        qi3 = qi.reshape(bs, IDX_HEADS, HEAD_DIM)
        w3 = misc[:, MISC_IW:MISC_IW + IDX_HEADS].reshape(bs, IDX_HEADS, 1)
        scores = _dsa_sample_scores(page_table, cache_kidx_t, li, qi3, w3)
        thr, self_sel = _dsa_sample_threshold(scores.reshape(bs, past), qi3, ki.reshape(bs, 1, HEAD_DIM),
                                              w3, topk_s)
        q3 = q.reshape(bs, DSA_Q_HEADS, HEAD_DIM)[:, head_perm].reshape(bs, DSA_Q_HEADS, 1, HEAD_DIM)
        qbd = jnp.where(head_of_row == kv_of_lane,
                        jnp.tile(q3, (1, 1, DSA_KV_HEADS, 1)).reshape(bs, DSA_Q_HEADS, -1), 0).astype(BF16)
        o_dsa = _dsa_sample_attention(page_table, cache_k2, cache_v2, li, qbd, scores,
                                      thr.reshape(bs, 1, 1), self_sel.reshape(bs, 1, 1),
                                      k.reshape(bs, 1, -1), v.reshape(bs, 1, -1))
        o_dsa = o_dsa[:, head_perm].reshape(bs, 512)
        x1t, base, route, cnt = _post_mix(o_lin, o_dsa, xs, p_sample[li].reshape(bs, -1), lw['woa'],
                                          lw['wod_p'], lw['ln1_g'], lw['ln1_b'], lw['wr'], lw['br'],
                                          lw['wpg'], lw['wpp'], zero_cnt, bs)
        xs = _moe_layer(x1t, route, cnt, base, lw, ew, li, bs)
        outs['ks'].append(k.reshape(bs, ts, DSA_KV_HEADS, HEAD_DIM))
        outs['vs'].append(v.reshape(bs, ts, DSA_KV_HEADS, HEAD_DIM))
        outs['kis'].append(ki.reshape(bs, ts, HEAD_DIM))
        outs['sgs'].append(_state_from_blockdiag(st[:, 0:2]))
        outs['shs'].append(_state_from_blockdiag(st[:, 2:4]))

    stack = lambda name: jnp.stack(outs[name], 0)
    return (xp.reshape(bp, t, d), xs.reshape(bs, ts, d),
            stack('kp'), stack('vp'), stack('kip'), stack('sgp'), stack('shp'),
            stack('ks'), stack('vs'), stack('kis'), stack('sgs'), stack('shs'))
```

```python
import functools

import jax
import jax.numpy as jnp
from jax import lax
from jax.experimental import pallas as pl
from jax.experimental.pallas import tpu as pltpu

F32 = jnp.float32
BF16 = jnp.bfloat16
I32 = jnp.int32

LANES = 128
SUBLANES = 8
VMEM_LIMIT = 56 * 1024 * 1024

D_MODEL = 1024
HEAD_DIM = 64
PAGE_SIZE = 128
IDX_HEADS = 8
N_EXPERTS = 32
TOP_K = 4
TOPK_MAX = 256
ROPE_DIM = 16
ROPE_THETA = 500000.0
GLA_GATE_TAU = 16.0
SWIGLU_ALPHA = 1.702
SWIGLU_LIMIT = 7.0
LN_EPS = 1e-5
DEPTH = 4
DEEPNORM_ALPHA = (2 * DEPTH) ** 0.25

_OFF = dict(gq=0, gk=256, gv=512, gr=768, glr=1024, hq=1040, hf=1296, hi=1552, hgt=1808,
            dq=2064, dk=2576, dv=2832, iq=3088, iw=3600, ik=3608)
PROJ_W = 3712
MISC_IW = 80
MISC_GLR = 88
NEG_BIG = -1e30

GLA_CHUNK = 128
GLA_SUB = 16
GLA_EXP_CLAMP = 60.0

MOE_BLK = 256
MOE_BLK_LOG2 = 8
TOP_K_LOG2 = 2
assert MOE_BLK == 1 << MOE_BLK_LOG2 and TOP_K == 1 << TOP_K_LOG2
DMA_UNROLL = 8
DMA_WAIT_UNROLL = 32


def _cparams(sem, vmem=VMEM_LIMIT):
    return pltpu.CompilerParams(dimension_semantics=sem, vmem_limit_bytes=vmem)


def _dot(a, b, precision=None):
    return jnp.dot(a, b, preferred_element_type=F32, precision=precision)


def _dot_nt(a, b, precision=None):
    return lax.dot_general(a, b, (((1,), (1,)), ((), ())), preferred_element_type=F32,
                           precision=precision)


def _sigmoid(x):
    return 1.0 / (1.0 + jnp.exp(-x))


def _log_sigmoid(x):
    return jnp.minimum(x, 0.0) - jnp.log(1.0 + jnp.exp(-jnp.abs(x)))


def _layer_norm_rows(x, g, b):
    mu = jnp.mean(x, axis=-1, keepdims=True)
    xc = x - mu
    var = jnp.mean(xc * xc, axis=-1, keepdims=True)
    return xc * lax.rsqrt(var + LN_EPS) * g + b


def _lower_bound_kernel(gam_ref, lb_ref):
    g = gam_ref[...]
    m = jnp.max(g, axis=0, keepdims=True)
    e = jnp.exp(g - m)
    sm = e / jnp.sum(e, axis=0, keepdims=True)
    acc = jnp.zeros_like(sm[0:1])
    for li in range(g.shape[0]):
        acc = acc + sm[li:li + 1]
        lb_ref[li:li + 1, :] = jnp.maximum(acc - sm[0:1], 0.0)


def _lower_bounds(hg_gamma):
    return pl.pallas_call(
        _lower_bound_kernel, out_shape=jax.ShapeDtypeStruct(hg_gamma.shape, F32),
        name="hgrn_lower_bounds")(hg_gamma.astype(F32))


def _ln_kernel(x_ref, g_ref, b_ref, o_ref):
    o_ref[...] = _layer_norm_rows(x_ref[...], g_ref[...], b_ref[...])


def _input_layer_norm(x, g, b, tm):
    n, d = x.shape
    return pl.pallas_call(
        _ln_kernel, grid=(n // tm,),
        in_specs=[pl.BlockSpec((tm, d), lambda i: (i, 0)),
                  pl.BlockSpec((1, d), lambda i: (0, 0)),
                  pl.BlockSpec((1, d), lambda i: (0, 0))],
        out_specs=pl.BlockSpec((tm, d), lambda i: (i, 0)),
        out_shape=jax.ShapeDtypeStruct((n, d), F32),
        compiler_params=_cparams(("parallel",)), name="input_ln")(x, g.reshape(1, d), b.reshape(1, d))


def _inproj_kernel(h_ref, w_ref, cos_ref, sa_ref, sb_ref,
                   g_ref, hh_ref, misc_ref, q_ref, qi_ref, k_ref, v_ref, ki_ref,
                   kb_ref, vb_ref, kid_ref):
    hb = h_ref[...].astype(BF16)
    cos, sa, sb = cos_ref[...], sa_ref[...], sb_ref[...]

    def proj(lo, width):
        return _dot(hb, w_ref[:, lo:lo + width])

    def rope(x):
        return x * cos + pltpu.roll(x, LANES - 8, 1) * sa + pltpu.roll(x, 8, 1) * sb

    g_ref[...] = proj(0, 1024)
    hh_ref[...] = proj(1024, 1024)
    for s in range(4):
        q_ref[:, s * LANES:(s + 1) * LANES] = (rope(proj(2048 + s * LANES, LANES)) * 0.125).astype(BF16)
        qi_ref[:, s * LANES:(s + 1) * LANES] = (rope(proj(3072 + s * LANES, LANES)) * 0.125).astype(BF16)
    for s in range(2):
        kr = rope(proj(2560 + s * LANES, LANES))
        k_ref[:, s * LANES:(s + 1) * LANES] = kr
        kb_ref[:, s * LANES:(s + 1) * LANES] = kr.astype(BF16)
    vv = proj(2816, 256)
    v_ref[...] = vv
    vb_ref[...] = vv.astype(BF16)
    misc = rope(proj(3584, LANES))
    misc_ref[...] = misc
    ki_ref[...] = misc[:, 0:HEAD_DIM]
    lane = lax.broadcasted_iota(I32, misc.shape, 1)
    ki_lo = jnp.where(lane < HEAD_DIM, misc, 0.0)
    kid_ref[...] = (ki_lo + pltpu.roll(ki_lo, HEAD_DIM, 1)).astype(BF16)


def _in_projection(h, w_r, cos, sa, sb, tm):
    n = h.shape[0]
    t_blocks = cos.shape[0] // tm
    row = lambda w: pl.BlockSpec((tm, w), lambda i: (i, 0))
    tab = pl.BlockSpec((tm, LANES), lambda i: (i % t_blocks, 0))
    widths = [(1024, F32), (1024, F32), (LANES, F32), (512, BF16), (512, BF16), (256, F32), (256, F32),
              (HEAD_DIM, F32), (256, BF16), (256, BF16), (LANES, BF16)]
    return pl.pallas_call(
        _inproj_kernel, grid=(n // tm,),
        in_specs=[row(D_MODEL), pl.BlockSpec((D_MODEL, PROJ_W), lambda i: (0, 0)), tab, tab, tab],
        out_specs=[row(w) for w, _ in widths],
        out_shape=[jax.ShapeDtypeStruct((n, w), dt) for w, dt in widths],
        compiler_params=_cparams(("parallel",)), name="in_projection")(h, w_r, cos, sa, sb)


def _cumsum_rows(x):
    n = x.shape[0]
    row = lax.broadcasted_iota(I32, x.shape, 0)
    sh = 1
    while sh < n:
        x = x + jnp.where(row >= sh, pltpu.roll(x, sh, 0), 0.0)
        sh *= 2
    return x


def _gla_kernel(g_ref, h_ref, misc_ref, wg_ref, bg_ref, lb_ref, gn_ref, hn_ref, s0_ref,
                o_ref, sT_ref, state, *, n_chunks, t_valid):
    c_idx = pl.program_id(1)

    @pl.when(c_idx == 0)
    def _():
        state[...] = s0_ref[0]

    ch = GLA_CHUNK
    n_sub = ch // GLA_SUB
    row = lax.broadcasted_iota(I32, (ch, ch), 0)
    col = lax.broadcasted_iota(I32, (ch, ch), 1)
    bd = jnp.where((row < HEAD_DIM) == (col < HEAD_DIM), 1.0, 0.0).astype(F32)
    lane = lax.broadcasted_iota(I32, (ch, LANES), 1)
    lane32 = lax.broadcasted_iota(I32, (2 * GLA_SUB, LANES), 1)
    row32 = lax.broadcasted_iota(I32, (2 * GLA_SUB, LANES), 0)
    head_sel = jnp.where((row32 < GLA_SUB) == (lane32 < HEAD_DIM), 1.0, 0.0).astype(F32)
    lane16 = lax.broadcasted_iota(I32, (GLA_SUB, LANES), 1)

    def chunk(ci, carry):
        r0 = pl.multiple_of(ci * ch, ch)
        rows = pl.ds(r0, ch)
        gx = g_ref[rows, :]
        hx = h_ref[rows, :]
        z = _dot(misc_ref[rows, :], wg_ref[...], precision=lax.Precision.HIGHEST) + bg_ref[...]
        lg_gla = _log_sigmoid(z) * (1.0 / GLA_GATE_TAU)
        lb = lb_ref[...]
        zf = hx[:, 256:512]
        a_ = jnp.log(lb)
        c_ = jnp.log(1.0 - lb) + _log_sigmoid(zf)
        lg_hg = jnp.maximum(a_, c_) + jnp.log(1.0 + jnp.exp(-jnp.abs(a_ - c_)))
        k_hg = (1.0 - lb) * _sigmoid(-zf)
        k_gla = gx[:, 256:512]
        if t_valid is not None:
            tpos = c_idx * (n_chunks * ch) + r0 + lax.broadcasted_iota(I32, (ch, 1), 0)
            live = tpos < t_valid
            lg_gla, lg_hg = jnp.where(live, lg_gla, 0.0), jnp.where(live, lg_hg, 0.0)
            k_gla, k_hg = jnp.where(live, k_gla, 0.0), jnp.where(live, k_hg, 0.0)
        hq = hx[:, 0:256]
        gr = gx[:, 768:1024]
        mix = ((gx[:, 0:256] * (HEAD_DIM ** -0.5), k_gla, gx[:, 512:768], _cumsum_rows(lg_gla),
                gr * _sigmoid(gr), gn_ref[...]),
               (hq * _sigmoid(hq), k_hg, hx[:, 512:768], _cumsum_rows(lg_hg),
                _sigmoid(hx[:, 768:1024]), hn_ref[...]))
        pairs = [(mi, p) for mi in range(2) for p in range(2)]
        sl = lambda p: slice(p * LANES, (p + 1) * LANES)

        prep = []
        for mi, p in pairs:
            qm, km, vm, bm = (mix[mi][j][:, sl(p)] for j in range(4))
            b_last = bm[ch - 1:ch, :]
            lhs, kss = [], []
            for i in range(n_sub):
                lo, n = i * GLA_SUB, (i + 1) * GLA_SUB
                ref = bm[lo - 1:lo, :] if i > 0 else jnp.zeros((1, LANES), F32)
                qs = qm[lo:n, :] * jnp.exp(bm[lo:n, :] - ref)
                lhs.append((jnp.concatenate([qs, qs], axis=0) * head_sel).astype(BF16))
                kss.append((km[0:n, :] * jnp.exp(jnp.minimum(ref - bm[0:n, :], GLA_EXP_CLAMP))).astype(BF16))
            prep.append(dict(qd=(qm * jnp.exp(bm)).astype(BF16), vb=vm.astype(BF16), vt=vm.T.astype(BF16),
                             kd=(km * jnp.exp(b_last - bm)).astype(BF16), decay=jnp.exp(b_last),
                             lhs=lhs, kss=kss))

        outs = []
        for idx, (mi, p) in enumerate(pairs):
            pr = prep[idx]
            sT = state[2 * mi + p]
            o_inter = _dot_nt(pr['qd'], sT.astype(BF16))
            scores = [_dot_nt(pr['lhs'][i], pr['kss'][i]) for i in range(n_sub)]
            upd = _dot(pr['vt'], pr['kd'])
            state[2 * mi + p] = (sT * pr['decay'] + upd) * bd
            outs.append((o_inter, scores))

        for idx, (mi, p) in enumerate(pairs):
            o_inter, scores = outs[idx]
            vb = prep[idx]['vb']
            parts = []
            for i in range(n_sub):
                lo, n = i * GLA_SUB, (i + 1) * GLA_SUB
                tt = lo + (lax.broadcasted_iota(I32, (2 * GLA_SUB, n), 0) % GLA_SUB)
                ss = lax.broadcasted_iota(I32, (2 * GLA_SUB, n), 1)
                a = jnp.where(ss <= tt, scores[i], 0.0).astype(BF16)
                oi = _dot(a, vb[0:n, :])
                parts.append(jnp.where(lane16 < HEAD_DIM, oi[0:GLA_SUB, :], oi[GLA_SUB:, :]))
            o = o_inter + jnp.concatenate(parts, axis=0)
            oo = o * o
            s_lo = jnp.sum(jnp.where(lane < HEAD_DIM, oo, 0.0), axis=1, keepdims=True)
            s_hi = jnp.sum(jnp.where(lane < HEAD_DIM, 0.0, oo), axis=1, keepdims=True)
            ms = jnp.where(lane < HEAD_DIM, s_lo, s_hi) * (1.0 / HEAD_DIM)
            on = o * lax.rsqrt(ms + LN_EPS) * mix[mi][5] * mix[mi][4][:, sl(p)]
            o_ref[rows, mi * 256 + p * LANES:mi * 256 + (p + 1) * LANES] = on
        return carry

    lax.fori_loop(0, n_chunks, chunk, 0)

    @pl.when(c_idx == pl.num_programs(1) - 1)
    def _():
        sT_ref[0] = state[...]


def _gla_hgrn(g_all, h_all, misc, wg_pad, bg, lb, gn, hn, s0_bd, batch, t_pad, t_valid):
    ct = min(2 * GLA_CHUNK, t_pad)
    steps = t_pad // ct
    n = batch * t_pad
    row = lambda w: pl.BlockSpec((ct, w), lambda b, c: (b * steps + c, 0))
    const = lambda shp: pl.BlockSpec(shp, lambda b, c: tuple(0 for _ in shp))
    st = pl.BlockSpec((1, 4, LANES, LANES), lambda b, c: (b, 0, 0, 0))
    kern = functools.partial(_gla_kernel, n_chunks=ct // GLA_CHUNK,
                             t_valid=None if t_valid == t_pad else t_valid)
    return pl.pallas_call(
        kern, grid=(batch, steps),
        in_specs=[row(1024), row(1024), row(LANES), const((LANES, 256)), const((1, 256)), const((1, 256)),
                  const((1, LANES)), const((1, LANES)), st],
        out_specs=[row(512), st],
        out_shape=[jax.ShapeDtypeStruct((n, 512), F32),
                   jax.ShapeDtypeStruct((batch, 4, LANES, LANES), F32)],
        scratch_shapes=[pltpu.VMEM((4, LANES, LANES), F32)],
        compiler_params=_cparams(("parallel", "arbitrary")), name="gla_hgrn")(
            g_all, h_all, misc, wg_pad, bg, lb, gn, hn, s0_bd)


def _sort_key(x):
    bits = lax.bitcast_convert_type(x, I32)
    return jnp.where(bits < 0, bits ^ jnp.int32(0x7FFFFFFF), bits)


def _kth_largest_key(count_ge, shape, k):
    int_min = jnp.int32(-2 ** 31)

    def body(it, cand):
        trial = cand | jnp.left_shift(jnp.int32(1), 31 - it)
        cnt = count_ge(trial ^ int_min)
        return jnp.where(cnt >= k, trial, cand)

    cand = lax.fori_loop(0, 32, body, jnp.zeros(shape, I32))
    return cand ^ int_min


def _topk_additive_mask(key_ref, neg_ref, w, topk, row_pos):
    qb = key_ref.shape[0]
    count = lambda mask: jnp.sum(jnp.where(mask, 1.0, 0.0), axis=1, keepdims=True)
    thr = _kth_largest_key(lambda t: count(key_ref[:, 0:w] >= t), (qb, 1), float(topk))
    col_pos = lax.broadcasted_iota(I32, (1, w), 1)
    keys = key_ref[:, 0:w]
    neg_ref[:, 0:w] = jnp.where((keys >= thr) & (col_pos <= row_pos), 0.0, NEG_BIG)

    need_eq = float(topk) - count(keys > thr)
    n_eq = count(keys == thr)
    above_masked = thr > _sort_key(jnp.full((1, 1), -jnp.inf, F32))
    excess = jnp.max(jnp.where(above_masked, n_eq - need_eq, 0.0))

    @pl.when(excess > 0.5)
    def _():
        r = lax.broadcasted_iota(I32, (LANES, LANES), 0)
        cc = lax.broadcasted_iota(I32, (LANES, LANES), 1)
        upper = jnp.where(r <= cc, 1.0, 0.0).astype(BF16)
        seen = jnp.zeros((qb, 1), F32)
        for c in range(w // LANES):
            cols = slice(c * LANES, (c + 1) * LANES)
            kt_ = key_ref[:, cols]
            eq = jnp.where(kt_ == thr, 1.0, 0.0)
            prefix = _dot(eq.astype(BF16), upper) + seen
            keep = (kt_ > thr) | ((kt_ == thr) & (prefix <= need_eq))
            cpos = c * LANES + lax.broadcasted_iota(I32, (1, LANES), 1)
            neg_ref[:, cols] = jnp.where(keep & (cpos <= row_pos), 0.0, NEG_BIG)
            seen = prefix[:, LANES - 1:LANES]


def _dsa_prompt_kernel(q_ref, qi_ref, misc_ref, kb_ref, vb_ref, kid_ref, o_ref,
                       sc_ref, key_ref, neg_ref, *, seq, kt, topk):
    qb = q_ref.shape[0]
    i = pl.program_id(1)
    q_lo = i * qb
    n_t = seq // kt
    n_act = (q_lo + qb - 1) // kt + 1
    row_pos = q_lo + lax.broadcasted_iota(I32, (qb, 1), 0)
    lane_q = lax.broadcasted_iota(I32, (qb, LANES), 1)
    half = (lane_q < HEAD_DIM, lane_q >= HEAD_DIM)
    w8 = misc_ref[...][:, MISC_IW:MISC_IW + IDX_HEADS] * (IDX_HEADS ** -0.5)

    sc_ref[...] = jnp.full(sc_ref.shape, -jnp.inf, F32)
    for t in range(n_t):
        @pl.when(t < n_act)
        def _(t=t):
            cols = slice(t * kt, (t + 1) * kt)
            kid = kid_ref[cols, :]
            acc = jnp.zeros((qb, kt), F32)
            for j in range(IDX_HEADS // 2):
                slab = qi_ref[:, j * LANES:(j + 1) * LANES]
                for c in range(2):
                    lhs = jnp.where(half[c], slab, jnp.zeros_like(slab))
                    logit = _dot_nt(lhs, kid)
                    acc = acc + jnp.maximum(logit, 0.0) * w8[:, 2 * j + c:2 * j + c + 1]
            col_pos = t * kt + lax.broadcasted_iota(I32, (1, kt), 1)
            sc_ref[:, cols] = jnp.where(col_pos <= row_pos, acc, -jnp.inf)
    key_ref[...] = _sort_key(sc_ref[...] + 0.0)

    for nt in range(1, n_t + 1):
        @pl.when(n_act == nt)
        def _(nt=nt):
            w = nt * kt
            _topk_additive_mask(key_ref, neg_ref, w, topk, row_pos)
            for p in range(2):
                kp = kb_ref[0:w, p * LANES:(p + 1) * LANES]
                vp = vb_ref[0:w, p * LANES:(p + 1) * LANES]
                for g in range(2):
                    s = 2 * p + g
                    qs = q_ref[:, s * LANES:(s + 1) * LANES]
                    outs = []
                    for c in range(2):
                        lhs = jnp.where(half[c], qs, jnp.zeros_like(qs))
                        sc = _dot_nt(lhs, kp) + neg_ref[:, 0:w]
                        m = jnp.max(sc, axis=1, keepdims=True)
                        pe = jnp.exp(sc - m)
                        l = jnp.sum(pe, axis=1, keepdims=True)
                        outs.append(_dot(pe.astype(BF16), vp) / l)
                    o_ref[:, s * LANES:(s + 1) * LANES] = jnp.where(half[0], outs[0], outs[1])


def _dsa_prompt(q, qi, misc, kb, vb, kid, batch, seq, qb=128):
    n = batch * seq
    nq = seq // qb
    kt = min(512, seq)
    topk = min(TOPK_MAX, seq // 4)
    rowq = lambda w: pl.BlockSpec((qb, w), lambda b, i: (b * nq + i, 0))
    full = lambda w: pl.BlockSpec((seq, w), lambda b, i: (b, 0))
    kern = functools.partial(_dsa_prompt_kernel, seq=seq, kt=kt, topk=topk)
    return pl.pallas_call(
        kern, grid=(batch, nq),
        in_specs=[rowq(512), rowq(512), rowq(LANES), full(256), full(256), full(LANES)],
        out_specs=rowq(512),
        out_shape=jax.ShapeDtypeStruct((n, 512), F32),
        scratch_shapes=[pltpu.VMEM((qb, seq), F32), pltpu.VMEM((qb, seq), I32), pltpu.VMEM((qb, seq), F32)],
        compiler_params=_cparams(("parallel", "arbitrary")), name="dsa_prompt")(q, qi, misc, kb, vb, kid)


PAGES_PER_STEP = 8
DSA_Q_HEADS = 8
DSA_KV_HEADS = 4


def _dsa_sample_score_kernel(pt_ref, qi_ref, w_ref, *refs):
    pages, out_ref = refs[:PAGES_PER_STEP], refs[PAGES_PER_STEP]
    qi = qi_ref[0]
    w = w_ref[0] * (IDX_HEADS ** -0.5)
    for j, pg in enumerate(pages):
        logit = _dot(qi, pg[...].astype(BF16))
        out_ref[0, j:j + 1, :] = jnp.sum(jnp.maximum(logit, 0.0) * w, axis=0, keepdims=True)


def _dsa_sample_scores(page_table, cache_kidx_t, li, qi3, w3):
    bd, n_pages = page_table.shape
    groups = n_pages // PAGES_PER_STEP

    def page_spec(j):
        return pl.BlockSpec((None, None, HEAD_DIM, PAGE_SIZE),
                            lambda b, g, pt: (li, pt[b, g * PAGES_PER_STEP + j], 0, 0))

    grid_spec = pltpu.PrefetchScalarGridSpec(
        num_scalar_prefetch=1, grid=(bd, groups),
        in_specs=[pl.BlockSpec((1, IDX_HEADS, HEAD_DIM), lambda b, g, pt: (b, 0, 0)),
                  pl.BlockSpec((1, IDX_HEADS, 1), lambda b, g, pt: (b, 0, 0))]
        + [page_spec(j) for j in range(PAGES_PER_STEP)],
        out_specs=pl.BlockSpec((1, PAGES_PER_STEP, PAGE_SIZE), lambda b, g, pt: (b, g, 0)))
    return pl.pallas_call(
        _dsa_sample_score_kernel, grid_spec=grid_spec,
        out_shape=jax.ShapeDtypeStruct((bd, n_pages, PAGE_SIZE), F32),
        compiler_params=_cparams(("parallel", "arbitrary")), name="dsa_sample_scores")(
            page_table, qi3, w3, *([cache_kidx_t] * PAGES_PER_STEP))


def _dsa_sample_threshold_kernel(sc_ref, qi_ref, ki_ref, w_ref, thr_ref, self_ref, key_ref, *, topk):
    qi = qi_ref[...].astype(F32)
    ki = ki_ref[...].astype(BF16).astype(F32)
    w = w_ref[...] * (IDX_HEADS ** -0.5)
    logit = jnp.sum(qi * ki, axis=2, keepdims=True)
    self_sc = jnp.sum(jnp.maximum(logit, 0.0) * w, axis=1)
    self_key = _sort_key(self_sc)
    key_ref[...] = _sort_key(sc_ref[...])

    def count_ge(thr):
        c = jnp.sum(jnp.where(key_ref[...] >= thr, 1.0, 0.0), axis=1, keepdims=True)
        return c + jnp.where(self_key >= thr, 1.0, 0.0)

    thr = _kth_largest_key(count_ge, self_key.shape, float(topk))
    thr_ref[...] = thr
    self_ref[...] = jnp.where(self_key >= thr, 1.0, 0.0)


def _dsa_sample_threshold(scores2, qi3, ki3, w3, topk):
    bd, past = scores2.shape
    kern = functools.partial(_dsa_sample_threshold_kernel, topk=topk)
    return pl.pallas_call(
        kern,
        out_shape=[jax.ShapeDtypeStruct((bd, 1), I32), jax.ShapeDtypeStruct((bd, 1), F32)],
        scratch_shapes=[pltpu.VMEM((bd, past), I32)],
        compiler_params=pltpu.CompilerParams(vmem_limit_bytes=VMEM_LIMIT),
        name="dsa_sample_threshold")(scores2, qi3, ki3, w3)


def _dsa_sample_attn_kernel(pt_ref, qbd_ref, sc_ref, thr_ref, self_ref, kn_ref, vn_ref, *refs,
                            n_groups):
    kpages = refs[:PAGES_PER_STEP]
    vpages = refs[PAGES_PER_STEP:2 * PAGES_PER_STEP]
    o_ref, m_ref, l_ref, acc_ref = refs[2 * PAGES_PER_STEP:]
    g = pl.program_id(1)
    nh = qbd_ref.shape[1]

    @pl.when(g == 0)
    def _():
        m_ref[...] = jnp.full(m_ref.shape, NEG_BIG, F32)
        l_ref[...] = jnp.zeros(l_ref.shape, F32)
        acc_ref[...] = jnp.zeros(acc_ref.shape, F32)

    qbd = qbd_ref[0]
    thr = thr_ref[0]

    for j in range(PAGES_PER_STEP):
        keep = _sort_key(sc_ref[0, j:j + 1, :]) >= thr
        sc = jnp.where(keep, _dot(qbd, kpages[j][...].astype(BF16)), NEG_BIG)
        m_old = m_ref[...]
        m_new = jnp.maximum(m_old, jnp.max(sc, axis=1, keepdims=True))
        alpha = jnp.exp(m_old - m_new)
        pe = jnp.where(keep, jnp.exp(sc - m_new), 0.0)
        l_ref[...] = alpha * l_ref[...] + jnp.sum(pe, axis=1, keepdims=True)
        acc_ref[...] = alpha * acc_ref[...] + _dot_nt(pe.astype(BF16), vpages[j][...].astype(BF16))
        m_ref[...] = m_new

    @pl.when(g == n_groups - 1)
    def _():
        kn = kn_ref[0].astype(BF16).astype(F32)
        sc = jnp.sum(qbd.astype(F32) * kn, axis=1, keepdims=True)
        sc = jnp.where(self_ref[0] > 0.5, sc, NEG_BIG)
        m_old = m_ref[...]
        m_new = jnp.maximum(m_old, sc)
        alpha = jnp.exp(m_old - m_new)
        pe = jnp.exp(sc - m_new)
        pe = jnp.where(sc > 0.5 * NEG_BIG, pe, 0.0)
        l = alpha * l_ref[...] + pe
        acc = alpha * acc_ref[...] + pe * vn_ref[0].astype(BF16).astype(F32)
        res = acc / l
        kvh = lax.broadcasted_iota(I32, (nh, HEAD_DIM), 0) // 2
        out = jnp.zeros((nh, HEAD_DIM), F32)
        for c in range(4):
            out = out + jnp.where(kvh == c, res[:, c * HEAD_DIM:(c + 1) * HEAD_DIM], 0.0)
        o_ref[0] = out


def _dsa_sample_attention(page_table, cache_k2, cache_v2, li, qbd, scores3, thr3, self3, kn3, vn3):
    bd, n_pages = page_table.shape
    groups = n_pages // PAGES_PER_STEP
    kvw = cache_k2.shape[-2]

    def page_spec(j):
        return pl.BlockSpec((None, None, kvw, PAGE_SIZE),
                            lambda b, g, pt: (li, pt[b, g * PAGES_PER_STEP + j], 0, 0))

    per_b = lambda shp: pl.BlockSpec((1,) + shp, lambda b, g, pt: (b, 0, 0))
    grid_spec = pltpu.PrefetchScalarGridSpec(
        num_scalar_prefetch=1, grid=(bd, groups),
        in_specs=[per_b((DSA_Q_HEADS, kvw)),
                  pl.BlockSpec((1, PAGES_PER_STEP, PAGE_SIZE), lambda b, g, pt: (b, g, 0)),
                  per_b((1, 1)), per_b((1, 1)), per_b((1, kvw)), per_b((1, kvw))]
        + [page_spec(j) for j in range(PAGES_PER_STEP)] * 2,
        out_specs=per_b((DSA_Q_HEADS, HEAD_DIM)),
        scratch_shapes=[pltpu.VMEM((DSA_Q_HEADS, 1), F32), pltpu.VMEM((DSA_Q_HEADS, 1), F32),
                        pltpu.VMEM((DSA_Q_HEADS, kvw), F32)])
    kern = functools.partial(_dsa_sample_attn_kernel, n_groups=groups)
    return pl.pallas_call(
        kern, grid_spec=grid_spec,
        out_shape=jax.ShapeDtypeStruct((bd, DSA_Q_HEADS, HEAD_DIM), F32),
        compiler_params=_cparams(("parallel", "arbitrary")), name="dsa_sample_attention")(
            page_table, qbd, scores3, thr3, self3, kn3, vn3,
            *([cache_k2] * PAGES_PER_STEP), *([cache_v2] * PAGES_PER_STEP))


def _post_mix_kernel(oa_ref, od_ref, x_ref, p_ref, woa_ref, wod_ref, g1_ref, b1_ref, wr_ref, br_ref,
                     wpg_ref, wpp_ref, cnt0_ref, x1t_ref, base_ref, route_ref, cnt_ref):
    tb = x_ref.shape[0]

    @pl.when(pl.program_id(0) == 0)
    def _():
        cnt_ref[...] = cnt0_ref[...]

    mixv = _dot(oa_ref[...].astype(BF16), woa_ref[...]) + _dot(od_ref[...].astype(BF16), wod_ref[...])
    x1 = _layer_norm_rows(DEEPNORM_ALPHA * x_ref[...] + mixv, g1_ref[...], b1_ref[...])
    x1b = x1.astype(BF16)
    for j in range(SUBLANES):
        x1t_ref[pl.ds(j, tb, stride=SUBLANES), :] = x1[:, j * LANES:(j + 1) * LANES]
    ple = _sigmoid(_dot(x1b, wpg_ref[...])) * _dot(p_ref[...].astype(BF16), wpp_ref[...])
    base_ref[...] = DEEPNORM_ALPHA * x1 + ple

    logits = _dot(x1, wr_ref[...], precision=lax.Precision.HIGHEST) + br_ref[...]
    lane = lax.broadcasted_iota(I32, (tb, LANES), 1)
    lanef = lane.astype(F32)
    vals, hots = [], []
    work = logits
    for _ in range(TOP_K):
        m = jnp.max(work, axis=1, keepdims=True)
        idx = jnp.min(jnp.where(work == m, lanef, float(LANES)), axis=1, keepdims=True)
        hot = lanef == idx
        vals.append(m)
        hots.append(hot)
        work = jnp.where(hot, -jnp.inf, work)
    exps = [jnp.exp(v - vals[0]) for v in vals]
    den = exps[0] + exps[1] + exps[2] + exps[3]
    oh = jnp.zeros((tb, LANES), F32)
    for hot in hots:
        oh = oh + jnp.where(hot, 1.0, 0.0)
    r = lax.broadcasted_iota(I32, (tb, tb), 0)
    c = lax.broadcasted_iota(I32, (tb, tb), 1)
    strict = jnp.where(c < r, 1.0, 0.0).astype(BF16)
    before = _dot(strict, oh.astype(BF16)) + cnt_ref[...]
    route = jnp.zeros((tb, LANES), F32)
    for k in range(TOP_K):
        gate = exps[k] / den
        eidx = jnp.sum(jnp.where(hots[k], lanef, 0.0), axis=1, keepdims=True)
        rank = jnp.sum(jnp.where(hots[k], before, 0.0), axis=1, keepdims=True)
        route = route + jnp.where(lane == k, gate, 0.0) + jnp.where(lane == TOP_K + k, eidx, 0.0) \
            + jnp.where(lane == 2 * TOP_K + k, rank, 0.0)
    route_ref[...] = route
    cnt_ref[...] = cnt_ref[...] + jnp.sum(oh, axis=0, keepdims=True)


def _post_mix(oa, od, x, p, woa, wod, g1, b1, wr, br, wpg, wpp, cnt0, tb):
    n = x.shape[0]
    row = lambda w: pl.BlockSpec((tb, w), lambda i: (i, 0))
    const = lambda a: pl.BlockSpec(a.shape, lambda i: tuple(0 for _ in a.shape))
    return pl.pallas_call(
        _post_mix_kernel, grid=(n // tb,),
        in_specs=[row(512), row(512), row(D_MODEL), row(p.shape[1]), const(woa), const(wod), const(g1),
                  const(b1), const(wr), const(br), const(wpg), const(wpp), const(cnt0)],
        out_specs=[pl.BlockSpec((tb * SUBLANES, LANES), lambda i: (i, 0)), row(D_MODEL), row(LANES),
                   pl.BlockSpec((1, LANES), lambda i: (0, 0))],
        out_shape=[jax.ShapeDtypeStruct((n * SUBLANES, LANES), F32), jax.ShapeDtypeStruct((n, D_MODEL), F32),
                   jax.ShapeDtypeStruct((n, LANES), F32), jax.ShapeDtypeStruct((1, LANES), F32)],
        compiler_params=_cparams(("arbitrary",)), name="post_mix_router")(
            oa, od, x, p, woa, wod, g1, b1, wr, br, wpg, wpp, cnt0)


def _slot_table_kernel(dest_ref, tab_ref):
    unroll = 16

    def clear(c, carry):
        for u in range(unroll):
            tab_ref[c * unroll + u] = jnp.int32(-1)
        return carry

    lax.fori_loop(0, tab_ref.shape[0] // unroll, clear, 0)

    def place(c, carry):
        for u in range(unroll):
            a = c * unroll + u
            tab_ref[dest_ref[a]] = a
        return carry

    lax.fori_loop(0, dest_ref.shape[0] // unroll, place, 0)


def _slot_table(dest, n_blocks):
    return pl.pallas_call(
        _slot_table_kernel,
        in_specs=[pl.BlockSpec(memory_space=pltpu.SMEM)],
        out_specs=pl.BlockSpec(memory_space=pltpu.SMEM),
        out_shape=jax.ShapeDtypeStruct((n_blocks * MOE_BLK,), I32),
        name="moe_slot_table")(dest.reshape(-1))


def _fused_row_copy(src, src_row, dst, dst_row, sem):
    return pltpu.make_async_copy(src.at[pl.ds(src_row * SUBLANES, SUBLANES), :],
                                 dst.at[pl.ds(dst_row * SUBLANES, SUBLANES), :], sem)


def _moe_fused_kernel(be_ref, nu_ref, tab_cur, tab_nxt, x_hbm, wg_ref, bg_ref, wu_ref, bu_ref, wd_ref,
                      bd_ref, y_hbm, xbuf, ybuf, gsem, ssem, wgb, wub, wdb, *, n_tok):
    i = pl.program_id(0)
    n_used = nu_ref[0]
    slot = i % 2
    dump_row = TOP_K * n_tok

    def wait_rows(sem):
        def drain(c, carry):
            for _ in range(DMA_WAIT_UNROLL):
                _fused_row_copy(x_hbm, 0, xbuf.at[0], 0, sem).wait()
            return carry

        lax.fori_loop(0, MOE_BLK // DMA_WAIT_UNROLL, drain, 0)

    def gather(tab, dst_slot):
        def issue(c, carry):
            for u in range(DMA_UNROLL):
                s = c * DMA_UNROLL + u
                tok = lax.shift_right_logical(jnp.maximum(tab[0, 0, s], 0), TOP_K_LOG2)
                _fused_row_copy(x_hbm, tok, xbuf.at[dst_slot], s, gsem.at[dst_slot]).start()
            return carry

        lax.fori_loop(0, MOE_BLK // DMA_UNROLL, issue, 0)

    def scatter(tab, src_slot):
        def issue(c, carry):
            for u in range(DMA_UNROLL):
                s = c * DMA_UNROLL + u
                a = tab[0, 0, s]
                row = jnp.where(a < 0, dump_row, (a & (TOP_K - 1)) * n_tok + lax.shift_right_logical(a, TOP_K_LOG2))
                _fused_row_copy(ybuf.at[src_slot], s, y_hbm, row, ssem.at[src_slot]).start()
            return carry

        lax.fori_loop(0, MOE_BLK // DMA_UNROLL, issue, 0)

    @pl.when(i < n_used)
    def _():
        @pl.when(i == 0)
        def _():
            gather(tab_cur, 0)

        @pl.when(i + 1 < n_used)
        def _():
            gather(tab_nxt, 1 - slot)

        wait_rows(gsem.at[slot])

        @pl.when(i >= 2)
        def _():
            wait_rows(ssem.at[slot])

        prev = be_ref[jnp.maximum(i - 1, 0)]

        @pl.when((i == 0) | (prev != be_ref[i]))
        def _():
            wgb[...] = wg_ref[...].astype(BF16)
            wub[...] = wu_ref[...].astype(BF16)
            wdb[...] = wd_ref[...].astype(BF16)

        x = jnp.concatenate([xbuf[slot, pl.ds(j, MOE_BLK, stride=SUBLANES), :] for j in range(SUBLANES)],
                            axis=1).astype(BF16)
        g = jnp.minimum(_dot(x, wgb[...]) + bg_ref[...], SWIGLU_LIMIT)
        u = jnp.clip(_dot(x, wub[...]) + bu_ref[...], -SWIGLU_LIMIT, SWIGLU_LIMIT)
        hdn = (u + 1.0) * g * _sigmoid(SWIGLU_ALPHA * g)
        y = _dot(hdn.astype(BF16), wdb[...]) + bd_ref[...]
        for j in range(SUBLANES):
            ybuf[slot, pl.ds(j, MOE_BLK, stride=SUBLANES), :] = y[:, j * LANES:(j + 1) * LANES]
        scatter(tab_cur, slot)

        @pl.when(i == n_used - 1)
        def _():
            wait_rows(ssem.at[slot])

            @pl.when(i >= 1)
            def _():
                wait_rows(ssem.at[1 - slot])

            spare = _fused_row_copy(ybuf.at[slot], 0, y_hbm, dump_row, ssem.at[slot])
            spare.start()
            spare.wait()


def _moe_fused(block_expert, n_used, tab3, x1t, li, wg, bg, wu, bu, wd, bd):
    n_blocks = block_expert.shape[0]
    n_tok = x1t.shape[0] // SUBLANES
    d, f = wg.shape[2], wg.shape[3]

    def w_idx(i, be, nu):
        return (li, be[jnp.minimum(i, nu[0] - 1)], 0, 0)

    tab_spec = lambda step: pl.BlockSpec(
        (1, 1, MOE_BLK), lambda i, be, nu: (jnp.minimum(i + step, nu[0] - 1), 0, 0), memory_space=pltpu.SMEM)
    grid_spec = pltpu.PrefetchScalarGridSpec(
        num_scalar_prefetch=2, grid=(n_blocks,),
        in_specs=[tab_spec(0), tab_spec(1), pl.BlockSpec(memory_space=pl.ANY),
                  pl.BlockSpec((None, None, d, f), w_idx), pl.BlockSpec((None, None, 1, f), w_idx),
                  pl.BlockSpec((None, None, d, f), w_idx), pl.BlockSpec((None, None, 1, f), w_idx),
                  pl.BlockSpec((None, None, f, d), w_idx), pl.BlockSpec((None, None, 1, d), w_idx)],
        out_specs=pl.BlockSpec(memory_space=pl.ANY),
        scratch_shapes=[pltpu.VMEM((2, MOE_BLK * SUBLANES, LANES), F32),
                        pltpu.VMEM((2, MOE_BLK * SUBLANES, LANES), F32),
                        pltpu.SemaphoreType.DMA((2,)), pltpu.SemaphoreType.DMA((2,)),
                        pltpu.VMEM((d, f), BF16), pltpu.VMEM((d, f), BF16), pltpu.VMEM((f, d), BF16)])
    kern = functools.partial(_moe_fused_kernel, n_tok=n_tok)
    return pl.pallas_call(
        kern, grid_spec=grid_spec,
        out_shape=jax.ShapeDtypeStruct(((TOP_K * n_tok + 1) * SUBLANES, LANES), F32),
        compiler_params=_cparams(("arbitrary",)), name="moe_experts")(
            block_expert, n_used, tab3, tab3, x1t, wg, bg, wu, bu, wd, bd)


def _moe_sum_kernel(y0_ref, y1_ref, y2_ref, y3_ref, route_ref, base_ref, g2_ref, b2_ref, o_ref):
    tb = base_ref.shape[0]
    gates = route_ref[...][:, 0:TOP_K]
    cols = []
    for j in range(SUBLANES):
        acc = jnp.zeros((tb, LANES), F32)
        for k, y_ref in enumerate((y0_ref, y1_ref, y2_ref, y3_ref)):
            acc = acc + gates[:, k:k + 1] * y_ref[pl.ds(j, tb, stride=SUBLANES), :]
        cols.append(acc)
    ffn = jnp.concatenate(cols, axis=1)
    o_ref[...] = _layer_norm_rows(base_ref[...] + ffn, g2_ref[...], b2_ref[...])


def _moe_sum(y_tok, route, base, g2, b2, tb):
    n = base.shape[0]
    steps = n // tb
    row = lambda w: pl.BlockSpec((tb, w), lambda i: (i, 0))
    const = lambda a: pl.BlockSpec(a.shape, lambda i: tuple(0 for _ in a.shape))
    y_spec = lambda k: pl.BlockSpec((tb * SUBLANES, LANES), lambda i: (k * steps + i, 0))
    return pl.pallas_call(
        _moe_sum_kernel, grid=(steps,),
        in_specs=[y_spec(k) for k in range(TOP_K)] + [row(LANES), row(D_MODEL), const(g2), const(b2)],
        out_specs=row(D_MODEL),
        out_shape=jax.ShapeDtypeStruct((n, D_MODEL), F32),
        compiler_params=_cparams(("parallel",)), name="moe_combine")(
            *([y_tok] * TOP_K), route, base, g2, b2)


def _moe_layer(x1t, route, cnt, base, lw, ew, li, tb):
    n = base.shape[0]
    a = n * TOP_K
    n_blocks = a // MOE_BLK + N_EXPERTS
    counts = cnt[0, :N_EXPERTS].astype(I32)
    padded = (counts + MOE_BLK - 1) // MOE_BLK * MOE_BLK
    pad_end = jnp.cumsum(padded)
    pad_start = pad_end - padded
    top_idx = route[:, TOP_K:2 * TOP_K].astype(I32)
    rank = route[:, 2 * TOP_K:3 * TOP_K].astype(I32)
    dest = pad_start[top_idx] + rank
    block_start = jnp.arange(n_blocks, dtype=I32) * MOE_BLK
    block_expert = jnp.minimum(jnp.sum((pad_end[None, :] <= block_start[:, None]).astype(I32), axis=1),
                               N_EXPERTS - 1).astype(I32)
    n_used = (pad_end[-1:] // MOE_BLK).astype(I32)
    tab3 = _slot_table(dest, n_blocks).reshape(n_blocks, 1, MOE_BLK)
    y_tok = _moe_fused(block_expert, n_used, tab3, x1t, li, *ew)
    return _moe_sum(y_tok, route, base, lw['ln2_g'], lw['ln2_b'], tb)


_DQ_PROMPT_ORDER = (0, 2, 1, 3, 4, 6, 5, 7)


def _relayout_w_in(w, head_order):
    col = lambda name, width: w[:, _OFF[name]:_OFF[name] + width]
    dq = col('dq', 512).reshape(D_MODEL, 8, HEAD_DIM)[:, jnp.array(head_order), :].reshape(D_MODEL, 512)
    z = lambda width: jnp.zeros((D_MODEL, width), w.dtype)
    misc = jnp.concatenate([col('ik', 64), z(16), col('iw', 8), col('glr', 16), z(24)], axis=1)
    out = jnp.concatenate([col('gq', 256), col('gk', 256), col('gv', 256), col('gr', 256),
                           col('hq', 256), col('hf', 256), col('hi', 256), col('hgt', 256),
                           dq, col('dk', 256), col('dv', 256), col('iq', 512), misc], axis=1)
    return out.astype(BF16)


def _rope_tables(pos):
    half = ROPE_DIM // 2
    inv = ROPE_THETA ** (-jnp.arange(half, dtype=F32) / half)
    ang = pos.astype(F32)[:, None] * inv[None, :]
    cos, sin = jnp.cos(ang), jnp.sin(ang)
    t = pos.shape[0]
    one, zero = jnp.ones((t, HEAD_DIM - ROPE_DIM), F32), jnp.zeros((t, HEAD_DIM - ROPE_DIM), F32)
    z8 = jnp.zeros((t, half), F32)
    c64 = jnp.concatenate([cos, cos, one], axis=1)
    a64 = jnp.concatenate([-sin, z8, zero], axis=1)
    b64 = jnp.concatenate([z8, sin, zero], axis=1)
    tile2 = lambda a: jnp.concatenate([a, a], axis=1)
    return tile2(c64), tile2(a64), tile2(b64)


def _layer_weights(li, w):
    wo = w['w_out'][li]
    wod = wo[512:].reshape(8, HEAD_DIM, D_MODEL)
    wg_pad = jnp.zeros((LANES, 256), F32).at[MISC_GLR:MISC_GLR + 16].set(w['w_gla_gate'][li])
    wr = jnp.zeros((D_MODEL, LANES), F32).at[:, :N_EXPERTS].set(w['w_router'][li])
    br = jnp.full((1, LANES), NEG_BIG, F32).at[0, :N_EXPERTS].set(w['b_router'][li])
    tile2 = lambda g: jnp.concatenate([g, g]).reshape(1, LANES)
    return dict(
        w_in_p=_relayout_w_in(w['w_in'][li], _DQ_PROMPT_ORDER),
        wg_pad=wg_pad, bg=w['b_gla_gate'][li].reshape(1, 256),
        gn=tile2(w['gla_norm_g'][li]), hn=tile2(w['hg_norm_g'][li]),
        woa=wo[:512].astype(BF16),
        wod_p=wod[jnp.array(_DQ_PROMPT_ORDER)].reshape(512, D_MODEL).astype(BF16),
        ln1_g=w['ln1_g'][li].reshape(1, D_MODEL), ln1_b=w['ln1_b'][li].reshape(1, D_MODEL),
        wr=wr, br=br,
        wpg=w['w_ple_gate'][li].astype(BF16), wpp=w['w_ple_proj'][li].astype(BF16),
        ln2_g=w['ln2_g'][li].reshape(1, D_MODEL), ln2_b=w['ln2_b'][li].reshape(1, D_MODEL))


def _state_to_blockdiag(s):
    b = s.shape[0]
    st = jnp.swapaxes(s, -1, -2).reshape(b, 2, 2, HEAD_DIM, HEAD_DIM)
    z = jnp.zeros_like(st[:, :, 0])
    top = jnp.concatenate([st[:, :, 0], z], axis=-1)
    bot = jnp.concatenate([z, st[:, :, 1]], axis=-1)
    return jnp.concatenate([top, bot], axis=-2)


def _state_from_blockdiag(sbd):
    b = sbd.shape[0]
    h0 = sbd[:, :, :HEAD_DIM, :HEAD_DIM]
    h1 = sbd[:, :, HEAD_DIM:, HEAD_DIM:]
    st = jnp.stack([h0, h1], axis=2).reshape(b, 4, HEAD_DIM, HEAD_DIM)
    return jnp.swapaxes(st, -1, -2)


def kernel(x_prompt, x_sample, cache_k, cache_v, cache_kidx, state_gla, state_hgrn, page_table, p_prompt, p_sample, ln_in_g, ln_in_b, w_in, w_gla_gate, b_gla_gate, gla_norm_g, hg_gamma, hg_norm_g, w_out, ln1_g, ln1_b, w_router, b_router, w_exp_gate, b_exp_gate, w_exp_up, b_exp_up, w_exp_down, b_exp_down, ln2_g, ln2_b, w_ple_gate, w_ple_proj):
    bp, t, d = x_prompt.shape
    bs, ts, _ = x_sample.shape
    assert ts == 1 and d == D_MODEL
    depth = w_in.shape[0]
    n_pages = page_table.shape[1]
    past = n_pages * PAGE_SIZE
    n_p = bp * t
    weights = dict(w_in=w_in, w_gla_gate=w_gla_gate, b_gla_gate=b_gla_gate, gla_norm_g=gla_norm_g,
                   hg_norm_g=hg_norm_g, w_out=w_out, ln1_g=ln1_g, ln1_b=ln1_b, w_router=w_router,
                   b_router=b_router, w_exp_gate=w_exp_gate, b_exp_gate=b_exp_gate, w_exp_up=w_exp_up,
                   b_exp_up=b_exp_up, w_exp_down=w_exp_down, b_exp_down=b_exp_down, ln2_g=ln2_g,
                   ln2_b=ln2_b, w_ple_gate=w_ple_gate, w_ple_proj=w_ple_proj)

    tm_p = min(512, t)
    tb_p = min(256, t)
    lbs = _lower_bounds(hg_gamma)
    tabs_p = _rope_tables(jnp.arange(t, dtype=I32))
    tabs_s = _rope_tables(jnp.full((bs,), past, I32))
    xp = _input_layer_norm(x_prompt.reshape(n_p, d), ln_in_g, ln_in_b, tm_p)
    xs = _input_layer_norm(x_sample.reshape(bs, d), ln_in_g, ln_in_b, bs)
    pages_t = lambda c: jnp.transpose(c, (0, 1, 3, 4, 2)).reshape(c.shape[0], c.shape[1], -1, PAGE_SIZE)
    cache_k2, cache_v2 = pages_t(cache_k), pages_t(cache_v)
    cache_kidx_t = jnp.transpose(cache_kidx, (0, 1, 3, 2))
    bias4 = lambda b: b.reshape(b.shape[0], b.shape[1], 1, b.shape[2])
    ew = (w_exp_gate, bias4(b_exp_gate), w_exp_up, bias4(b_exp_up), w_exp_down, bias4(b_exp_down))
    head_perm = jnp.array(_DQ_PROMPT_ORDER)
    zero_state = jnp.zeros((bp, 4, LANES, LANES), F32)
    zero_cnt = jnp.zeros((1, LANES), F32)
    topk_s = min(TOPK_MAX, (past + ts) // 4)
    head_of_row = jnp.arange(DSA_Q_HEADS)[:, None] // 2
    kv_of_lane = jnp.arange(DSA_KV_HEADS * HEAD_DIM)[None, :] // HEAD_DIM

    outs = {k: [] for k in ('kp', 'vp', 'kip', 'sgp', 'shp', 'ks', 'vs', 'kis', 'sgs', 'shs')}
    for li in range(depth):
        lw = _layer_weights(li, weights)
        lb = lbs[li].reshape(1, 256)

        (g_all, h_all, misc, q, qi, k, v, ki, kb, vb, kid) = _in_projection(xp, lw['w_in_p'], *tabs_p, tm_p)
        o_lin, st = _gla_hgrn(g_all, h_all, misc, lw['wg_pad'], lw['bg'], lb, lw['gn'], lw['hn'],
                              zero_state, bp, t, t)
        o_dsa = _dsa_prompt(q, qi, misc, kb, vb, kid, bp, t)
        x1t, base, route, cnt = _post_mix(o_lin, o_dsa, xp, p_prompt[li].reshape(n_p, -1), lw['woa'],
                                          lw['wod_p'], lw['ln1_g'], lw['ln1_b'], lw['wr'], lw['br'],
                                          lw['wpg'], lw['wpp'], zero_cnt, tb_p)
        xp = _moe_layer(x1t, route, cnt, base, lw, ew, li, tb_p)
        outs['kp'].append(k.reshape(bp, t, DSA_KV_HEADS, HEAD_DIM))
        outs['vp'].append(v.reshape(bp, t, DSA_KV_HEADS, HEAD_DIM))
        outs['kip'].append(ki.reshape(bp, t, HEAD_DIM))
        outs['sgp'].append(_state_from_blockdiag(st[:, 0:2]))
        outs['shp'].append(_state_from_blockdiag(st[:, 2:4]))

        (g_all, h_all, misc, q, qi, k, v, ki, kb, vb, kid) = _in_projection(xs, lw['w_in_p'], *tabs_s, bs)
        pad = lambda a: jnp.pad(a.reshape(bs, 1, -1), ((0, 0), (0, GLA_CHUNK - 1), (0, 0))).reshape(
            bs * GLA_CHUNK, -1)
        s0 = jnp.concatenate([_state_to_blockdiag(state_gla[li]), _state_to_blockdiag(state_hgrn[li])], axis=1)
        o_lin, st = _gla_hgrn(pad(g_all), pad(h_all), pad(misc), lw['wg_pad'], lw['bg'], lb, lw['gn'],
                              lw['hn'], s0, bs, GLA_CHUNK, 1)
        o_lin = o_lin.reshape(bs, GLA_CHUNK, -1)[:, 0]
        qi3 = qi.reshape(bs, IDX_HEADS, HEAD_DIM)
        w3 = misc[:, MISC_IW:MISC_IW + IDX_HEADS].reshape(bs, IDX_HEADS, 1)
        scores = _dsa_sample_scores(page_table, cache_kidx_t, li, qi3, w3)
        thr, self_sel = _dsa_sample_threshold(scores.reshape(bs, past), qi3, ki.reshape(bs, 1, HEAD_DIM),
                                              w3, topk_s)
        q3 = q.reshape(bs, DSA_Q_HEADS, HEAD_DIM)[:, head_perm].reshape(bs, DSA_Q_HEADS, 1, HEAD_DIM)
        qbd = jnp.where(head_of_row == kv_of_lane,
                        jnp.tile(q3, (1, 1, DSA_KV_HEADS, 1)).reshape(bs, DSA_Q_HEADS, -1), 0).astype(BF16)
        o_dsa = _dsa_sample_attention(page_table, cache_k2, cache_v2, li, qbd, scores,
                                      thr.reshape(bs, 1, 1), self_sel.reshape(bs, 1, 1),
                                      k.reshape(bs, 1, -1), v.reshape(bs, 1, -1))
        o_dsa = o_dsa[:, head_perm].reshape(bs, 512)
        x1t, base, route, cnt = _post_mix(o_lin, o_dsa, xs, p_sample[li].reshape(bs, -1), lw['woa'],
                                          lw['wod_p'], lw['ln1_g'], lw['ln1_b'], lw['wr'], lw['br'],
                                          lw['wpg'], lw['wpp'], zero_cnt, bs)
        xs = _moe_layer(x1t, route, cnt, base, lw, ew, li, bs)
        outs['ks'].append(k.reshape(bs, ts, DSA_KV_HEADS, HEAD_DIM))
        outs['vs'].append(v.reshape(bs, ts, DSA_KV_HEADS, HEAD_DIM))
        outs['kis'].append(ki.reshape(bs, ts, HEAD_DIM))
        outs['sgs'].append(_state_from_blockdiag(st[:, 0:2]))
        outs['shs'].append(_state_from_blockdiag(st[:, 2:4]))

    stack = lambda name: jnp.stack(outs[name], 0)
    return (xp.reshape(bp, t, d), xs.reshape(bs, ts, d),
            stack('kp'), stack('vp'), stack('kip'), stack('sgp'), stack('shp'),
            stack('ks'), stack('vs'), stack('kis'), stack('sgs'), stack('shs'))
```

```python
import functools

import jax
import jax.numpy as jnp
from jax import lax
from jax.experimental import pallas as pl
from jax.experimental.pallas import tpu as pltpu

F32 = jnp.float32
BF16 = jnp.bfloat16
I32 = jnp.int32

LANES = 128
SUBLANES = 8
VMEM_LIMIT = 56 * 1024 * 1024

D_MODEL = 1024
HEAD_DIM = 64
PAGE_SIZE = 128
IDX_HEADS = 8
N_EXPERTS = 32
TOP_K = 4
TOPK_MAX = 256
ROPE_DIM = 16
ROPE_THETA = 500000.0
GLA_GATE_TAU = 16.0
SWIGLU_ALPHA = 1.702
SWIGLU_LIMIT = 7.0
LN_EPS = 1e-5
DEPTH = 4
DEEPNORM_ALPHA = (2 * DEPTH) ** 0.25

_OFF = dict(gq=0, gk=256, gv=512, gr=768, glr=1024, hq=1040, hf=1296, hi=1552, hgt=1808,
            dq=2064, dk=2576, dv=2832, iq=3088, iw=3600, ik=3608)
PROJ_W = 3712
MISC_IW = 80
MISC_GLR = 88
NEG_BIG = -1e30

GLA_CHUNK = 128
GLA_SUB = 16
GLA_EXP_CLAMP = 60.0

MOE_BLK = 256
MOE_BLK_LOG2 = 8
TOP_K_LOG2 = 2
assert MOE_BLK == 1 << MOE_BLK_LOG2 and TOP_K == 1 << TOP_K_LOG2
DMA_UNROLL = 8
DMA_UNROLL_LOG2 = 3
assert DMA_UNROLL == 1 << DMA_UNROLL_LOG2


def _cparams(sem, vmem=VMEM_LIMIT):
    return pltpu.CompilerParams(dimension_semantics=sem, vmem_limit_bytes=vmem)


def _dot(a, b, precision=None):
    return jnp.dot(a, b, preferred_element_type=F32, precision=precision)


def _dot_nt(a, b, precision=None):
    return lax.dot_general(a, b, (((1,), (1,)), ((), ())), preferred_element_type=F32,
                           precision=precision)


def _sigmoid(x):
    return 1.0 / (1.0 + jnp.exp(-x))


def _log_sigmoid(x):
    return jnp.minimum(x, 0.0) - jnp.log(1.0 + jnp.exp(-jnp.abs(x)))


def _layer_norm_rows(x, g, b):
    mu = jnp.mean(x, axis=-1, keepdims=True)
    xc = x - mu
    var = jnp.mean(xc * xc, axis=-1, keepdims=True)
    return xc * lax.rsqrt(var + LN_EPS) * g + b


def _lower_bound_kernel(gam_ref, lb_ref):
    g = gam_ref[...]
    m = jnp.max(g, axis=0, keepdims=True)
    e = jnp.exp(g - m)
    sm = e / jnp.sum(e, axis=0, keepdims=True)
    acc = jnp.zeros_like(sm[0:1])
    for li in range(g.shape[0]):
        acc = acc + sm[li:li + 1]
        lb_ref[li:li + 1, :] = jnp.maximum(acc - sm[0:1], 0.0)


def _lower_bounds(hg_gamma):
    return pl.pallas_call(
        _lower_bound_kernel, out_shape=jax.ShapeDtypeStruct(hg_gamma.shape, F32),
        name="hgrn_lower_bounds")(hg_gamma.astype(F32))


def _ln_kernel(x_ref, g_ref, b_ref, o_ref):
    o_ref[...] = _layer_norm_rows(x_ref[...], g_ref[...], b_ref[...])


def _input_layer_norm(x, g, b, tm):
    n, d = x.shape
    return pl.pallas_call(
        _ln_kernel, grid=(n // tm,),
        in_specs=[pl.BlockSpec((tm, d), lambda i: (i, 0)),
                  pl.BlockSpec((1, d), lambda i: (0, 0)),
                  pl.BlockSpec((1, d), lambda i: (0, 0))],
        out_specs=pl.BlockSpec((tm, d), lambda i: (i, 0)),
        out_shape=jax.ShapeDtypeStruct((n, d), F32),
        compiler_params=_cparams(("parallel",)), name="input_ln")(x, g.reshape(1, d), b.reshape(1, d))


def _inproj_kernel(h_ref, w_ref, cos_ref, sa_ref, sb_ref,
                   g_ref, hh_ref, misc_ref, q_ref, qi_ref, k_ref, v_ref, ki_ref,
                   kb_ref, vb_ref, kid_ref):
    hb = h_ref[...].astype(BF16)
    cos, sa, sb = cos_ref[...], sa_ref[...], sb_ref[...]

    def proj(lo, width):
        return _dot(hb, w_ref[:, lo:lo + width])

    def rope(x):
        return x * cos + pltpu.roll(x, LANES - 8, 1) * sa + pltpu.roll(x, 8, 1) * sb

    g_ref[...] = proj(0, 1024)
    hh_ref[...] = proj(1024, 1024)
    for s in range(4):
        q_ref[:, s * LANES:(s + 1) * LANES] = (rope(proj(2048 + s * LANES, LANES)) * 0.125).astype(BF16)
        qi_ref[:, s * LANES:(s + 1) * LANES] = (rope(proj(3072 + s * LANES, LANES)) * 0.125).astype(BF16)
    for s in range(2):
        kr = rope(proj(2560 + s * LANES, LANES))
        k_ref[:, s * LANES:(s + 1) * LANES] = kr
        kb_ref[:, s * LANES:(s + 1) * LANES] = kr.astype(BF16)
    vv = proj(2816, 256)
    v_ref[...] = vv
    vb_ref[...] = vv.astype(BF16)
    misc = rope(proj(3584, LANES))
    misc_ref[...] = misc
    ki_ref[...] = misc[:, 0:HEAD_DIM]
    lane = lax.broadcasted_iota(I32, misc.shape, 1)
    ki_lo = jnp.where(lane < HEAD_DIM, misc, 0.0)
    kid_ref[...] = (ki_lo + pltpu.roll(ki_lo, HEAD_DIM, 1)).astype(BF16)


def _in_projection(h, w_r, cos, sa, sb, tm):
    n = h.shape[0]
    t_blocks = cos.shape[0] // tm
    row = lambda w: pl.BlockSpec((tm, w), lambda i: (i, 0))
    tab = pl.BlockSpec((tm, LANES), lambda i: (i % t_blocks, 0))
    widths = [(1024, F32), (1024, F32), (LANES, F32), (512, BF16), (512, BF16), (256, F32), (256, F32),
              (HEAD_DIM, F32), (256, BF16), (256, BF16), (LANES, BF16)]
    return pl.pallas_call(
        _inproj_kernel, grid=(n // tm,),
        in_specs=[row(D_MODEL), pl.BlockSpec((D_MODEL, PROJ_W), lambda i: (0, 0)), tab, tab, tab],
        out_specs=[row(w) for w, _ in widths],
        out_shape=[jax.ShapeDtypeStruct((n, w), dt) for w, dt in widths],
        compiler_params=_cparams(("parallel",)), name="in_projection")(h, w_r, cos, sa, sb)


def _cumsum_rows(x):
    n = x.shape[0]
    row = lax.broadcasted_iota(I32, x.shape, 0)
    sh = 1
    while sh < n:
        x = x + jnp.where(row >= sh, pltpu.roll(x, sh, 0), 0.0)
        sh *= 2
    return x


def _gla_kernel(g_ref, h_ref, misc_ref, wg_ref, bg_ref, lb_ref, gn_ref, hn_ref, s0_ref,
                o_ref, sT_ref, state, *, n_chunks, t_valid):
    c_idx = pl.program_id(1)

    @pl.when(c_idx == 0)
    def _():
        state[...] = s0_ref[0]

    ch = GLA_CHUNK
    n_sub = ch // GLA_SUB
    row = lax.broadcasted_iota(I32, (ch, ch), 0)
    col = lax.broadcasted_iota(I32, (ch, ch), 1)
    bd = jnp.where((row < HEAD_DIM) == (col < HEAD_DIM), 1.0, 0.0).astype(F32)
    lane = lax.broadcasted_iota(I32, (ch, LANES), 1)
    lane32 = lax.broadcasted_iota(I32, (2 * GLA_SUB, LANES), 1)
    row32 = lax.broadcasted_iota(I32, (2 * GLA_SUB, LANES), 0)
    head_sel = jnp.where((row32 < GLA_SUB) == (lane32 < HEAD_DIM), 1.0, 0.0).astype(F32)
    lane16 = lax.broadcasted_iota(I32, (GLA_SUB, LANES), 1)

    def chunk(ci, carry):
        r0 = pl.multiple_of(ci * ch, ch)
        rows = pl.ds(r0, ch)
        gx = g_ref[rows, :]
        hx = h_ref[rows, :]
        z = _dot(misc_ref[rows, :], wg_ref[...], precision=lax.Precision.HIGHEST) + bg_ref[...]
        lg_gla = _log_sigmoid(z) * (1.0 / GLA_GATE_TAU)
        lb = lb_ref[...]
        zf = hx[:, 256:512]
        a_ = jnp.log(lb)
        c_ = jnp.log(1.0 - lb) + _log_sigmoid(zf)
        lg_hg = jnp.maximum(a_, c_) + jnp.log(1.0 + jnp.exp(-jnp.abs(a_ - c_)))
        k_hg = (1.0 - lb) * _sigmoid(-zf)
        k_gla = gx[:, 256:512]
        if t_valid is not None:
            tpos = c_idx * (n_chunks * ch) + r0 + lax.broadcasted_iota(I32, (ch, 1), 0)
            live = tpos < t_valid
            lg_gla, lg_hg = jnp.where(live, lg_gla, 0.0), jnp.where(live, lg_hg, 0.0)
            k_gla, k_hg = jnp.where(live, k_gla, 0.0), jnp.where(live, k_hg, 0.0)
        hq = hx[:, 0:256]
        gr = gx[:, 768:1024]
        mix = ((gx[:, 0:256] * (HEAD_DIM ** -0.5), k_gla, gx[:, 512:768], _cumsum_rows(lg_gla),
                gr * _sigmoid(gr), gn_ref[...]),
               (hq * _sigmoid(hq), k_hg, hx[:, 512:768], _cumsum_rows(lg_hg),
                _sigmoid(hx[:, 768:1024]), hn_ref[...]))
        pairs = [(mi, p) for mi in range(2) for p in range(2)]
        sl = lambda p: slice(p * LANES, (p + 1) * LANES)

        prep = []
        for mi, p in pairs:
            qm, km, vm, bm = (mix[mi][j][:, sl(p)] for j in range(4))
            b_last = bm[ch - 1:ch, :]
            lhs, kss = [], []
            for i in range(n_sub):
                lo, n = i * GLA_SUB, (i + 1) * GLA_SUB
                ref = bm[lo - 1:lo, :] if i > 0 else jnp.zeros((1, LANES), F32)
                qs = qm[lo:n, :] * jnp.exp(bm[lo:n, :] - ref)
                lhs.append((jnp.concatenate([qs, qs], axis=0) * head_sel).astype(BF16))
                kss.append((km[0:n, :] * jnp.exp(jnp.minimum(ref - bm[0:n, :], GLA_EXP_CLAMP))).astype(BF16))
            prep.append(dict(qd=(qm * jnp.exp(bm)).astype(BF16), vb=vm.astype(BF16), vt=vm.T.astype(BF16),
                             kd=(km * jnp.exp(b_last - bm)).astype(BF16), decay=jnp.exp(b_last),
                             lhs=lhs, kss=kss))

        outs = []
        for idx, (mi, p) in enumerate(pairs):
            pr = prep[idx]
            sT = state[2 * mi + p]
            o_inter = _dot_nt(pr['qd'], sT.astype(BF16))
            scores = [_dot_nt(pr['lhs'][i], pr['kss'][i]) for i in range(n_sub)]
            upd = _dot(pr['vt'], pr['kd'])
            state[2 * mi + p] = (sT * pr['decay'] + upd) * bd
            outs.append((o_inter, scores))

        for idx, (mi, p) in enumerate(pairs):
            o_inter, scores = outs[idx]
            vb = prep[idx]['vb']
            parts = []
            for i in range(n_sub):
                lo, n = i * GLA_SUB, (i + 1) * GLA_SUB
                tt = lo + (lax.broadcasted_iota(I32, (2 * GLA_SUB, n), 0) % GLA_SUB)
                ss = lax.broadcasted_iota(I32, (2 * GLA_SUB, n), 1)
                a = jnp.where(ss <= tt, scores[i], 0.0).astype(BF16)
                oi = _dot(a, vb[0:n, :])
                parts.append(jnp.where(lane16 < HEAD_DIM, oi[0:GLA_SUB, :], oi[GLA_SUB:, :]))
            o = o_inter + jnp.concatenate(parts, axis=0)
            oo = o * o
            s_lo = jnp.sum(jnp.where(lane < HEAD_DIM, oo, 0.0), axis=1, keepdims=True)
            s_hi = jnp.sum(jnp.where(lane < HEAD_DIM, 0.0, oo), axis=1, keepdims=True)
            ms = jnp.where(lane < HEAD_DIM, s_lo, s_hi) * (1.0 / HEAD_DIM)
            on = o * lax.rsqrt(ms + LN_EPS) * mix[mi][5] * mix[mi][4][:, sl(p)]
            o_ref[rows, mi * 256 + p * LANES:mi * 256 + (p + 1) * LANES] = on
        return carry

    lax.fori_loop(0, n_chunks, chunk, 0)

    @pl.when(c_idx == pl.num_programs(1) - 1)
    def _():
        sT_ref[0] = state[...]


def _gla_hgrn(g_all, h_all, misc, wg_pad, bg, lb, gn, hn, s0_bd, batch, t_pad, t_valid):
    ct = min(2 * GLA_CHUNK, t_pad)
    steps = t_pad // ct
    n = batch * t_pad
    row = lambda w: pl.BlockSpec((ct, w), lambda b, c: (b * steps + c, 0))
    const = lambda shp: pl.BlockSpec(shp, lambda b, c: tuple(0 for _ in shp))
    st = pl.BlockSpec((1, 4, LANES, LANES), lambda b, c: (b, 0, 0, 0))
    kern = functools.partial(_gla_kernel, n_chunks=ct // GLA_CHUNK,
                             t_valid=None if t_valid == t_pad else t_valid)
    return pl.pallas_call(
        kern, grid=(batch, steps),
        in_specs=[row(1024), row(1024), row(LANES), const((LANES, 256)), const((1, 256)), const((1, 256)),
                  const((1, LANES)), const((1, LANES)), st],
        out_specs=[row(512), st],
        out_shape=[jax.ShapeDtypeStruct((n, 512), F32),
                   jax.ShapeDtypeStruct((batch, 4, LANES, LANES), F32)],
        scratch_shapes=[pltpu.VMEM((4, LANES, LANES), F32)],
        compiler_params=_cparams(("parallel", "arbitrary")), name="gla_hgrn")(
            g_all, h_all, misc, wg_pad, bg, lb, gn, hn, s0_bd)


def _sort_key(x):
    bits = lax.bitcast_convert_type(x, I32)
    return jnp.where(bits < 0, bits ^ jnp.int32(0x7FFFFFFF), bits)


def _kth_largest_key(count_ge, shape, k):
    int_min = jnp.int32(-2 ** 31)

    def body(it, cand):
        trial = cand | jnp.left_shift(jnp.int32(1), 31 - it)
        cnt = count_ge(trial ^ int_min)
        return jnp.where(cnt >= k, trial, cand)

    cand = lax.fori_loop(0, 32, body, jnp.zeros(shape, I32))
    return cand ^ int_min


def _topk_additive_mask(key_ref, neg_ref, w, topk, row_pos):
    qb = key_ref.shape[0]
    count = lambda mask: jnp.sum(jnp.where(mask, 1.0, 0.0), axis=1, keepdims=True)
    thr = _kth_largest_key(lambda t: count(key_ref[:, 0:w] >= t), (qb, 1), float(topk))
    col_pos = lax.broadcasted_iota(I32, (1, w), 1)
    keys = key_ref[:, 0:w]
    neg_ref[:, 0:w] = jnp.where((keys >= thr) & (col_pos <= row_pos), 0.0, NEG_BIG)

    need_eq = float(topk) - count(keys > thr)
    n_eq = count(keys == thr)
    above_masked = thr > _sort_key(jnp.full((1, 1), -jnp.inf, F32))
    excess = jnp.max(jnp.where(above_masked, n_eq - need_eq, 0.0))

    @pl.when(excess > 0.5)
    def _():
        r = lax.broadcasted_iota(I32, (LANES, LANES), 0)
        cc = lax.broadcasted_iota(I32, (LANES, LANES), 1)
        upper = jnp.where(r <= cc, 1.0, 0.0).astype(BF16)
        seen = jnp.zeros((qb, 1), F32)
        for c in range(w // LANES):
            cols = slice(c * LANES, (c + 1) * LANES)
            kt_ = key_ref[:, cols]
            eq = jnp.where(kt_ == thr, 1.0, 0.0)
            prefix = _dot(eq.astype(BF16), upper) + seen
            keep = (kt_ > thr) | ((kt_ == thr) & (prefix <= need_eq))
            cpos = c * LANES + lax.broadcasted_iota(I32, (1, LANES), 1)
            neg_ref[:, cols] = jnp.where(keep & (cpos <= row_pos), 0.0, NEG_BIG)
            seen = prefix[:, LANES - 1:LANES]


def _dsa_prompt_kernel(q_ref, qi_ref, misc_ref, kb_ref, vb_ref, kid_ref, o_ref,
                       sc_ref, key_ref, neg_ref, *, seq, kt, topk):
    qb = q_ref.shape[0]
    i = pl.program_id(1)
    q_lo = i * qb
    n_t = seq // kt
    n_act = (q_lo + qb - 1) // kt + 1
    row_pos = q_lo + lax.broadcasted_iota(I32, (qb, 1), 0)
    lane_q = lax.broadcasted_iota(I32, (qb, LANES), 1)
    half = (lane_q < HEAD_DIM, lane_q >= HEAD_DIM)
    w8 = misc_ref[...][:, MISC_IW:MISC_IW + IDX_HEADS] * (IDX_HEADS ** -0.5)

    sc_ref[...] = jnp.full(sc_ref.shape, -jnp.inf, F32)
    for t in range(n_t):
        @pl.when(t < n_act)
        def _(t=t):
            cols = slice(t * kt, (t + 1) * kt)
            kid = kid_ref[cols, :]
            acc = jnp.zeros((qb, kt), F32)
            for j in range(IDX_HEADS // 2):
                slab = qi_ref[:, j * LANES:(j + 1) * LANES]
                for c in range(2):
                    lhs = jnp.where(half[c], slab, jnp.zeros_like(slab))
                    logit = _dot_nt(lhs, kid)
                    acc = acc + jnp.maximum(logit, 0.0) * w8[:, 2 * j + c:2 * j + c + 1]
            col_pos = t * kt + lax.broadcasted_iota(I32, (1, kt), 1)
            sc_ref[:, cols] = jnp.where(col_pos <= row_pos, acc, -jnp.inf)
    key_ref[...] = _sort_key(sc_ref[...] + 0.0)

    for nt in range(1, n_t + 1):
        @pl.when(n_act == nt)
        def _(nt=nt):
            w = nt * kt
            _topk_additive_mask(key_ref, neg_ref, w, topk, row_pos)
            for p in range(2):
                kp = kb_ref[0:w, p * LANES:(p + 1) * LANES]
                vp = vb_ref[0:w, p * LANES:(p + 1) * LANES]
                for g in range(2):
                    s = 2 * p + g
                    qs = q_ref[:, s * LANES:(s + 1) * LANES]
                    outs = []
                    for c in range(2):
                        lhs = jnp.where(half[c], qs, jnp.zeros_like(qs))
                        sc = _dot_nt(lhs, kp) + neg_ref[:, 0:w]
                        m = jnp.max(sc, axis=1, keepdims=True)
                        pe = jnp.exp(sc - m)
                        l = jnp.sum(pe, axis=1, keepdims=True)
                        outs.append(_dot(pe.astype(BF16), vp) / l)
                    o_ref[:, s * LANES:(s + 1) * LANES] = jnp.where(half[0], outs[0], outs[1])


def _dsa_prompt(q, qi, misc, kb, vb, kid, batch, seq, qb=128):
    n = batch * seq
    nq = seq // qb
    kt = min(512, seq)
    topk = min(TOPK_MAX, seq // 4)
    rowq = lambda w: pl.BlockSpec((qb, w), lambda b, i: (b * nq + i, 0))
    full = lambda w: pl.BlockSpec((seq, w), lambda b, i: (b, 0))
    kern = functools.partial(_dsa_prompt_kernel, seq=seq, kt=kt, topk=topk)
    return pl.pallas_call(
        kern, grid=(batch, nq),
        in_specs=[rowq(512), rowq(512), rowq(LANES), full(256), full(256), full(LANES)],
        out_specs=rowq(512),
        out_shape=jax.ShapeDtypeStruct((n, 512), F32),
        scratch_shapes=[pltpu.VMEM((qb, seq), F32), pltpu.VMEM((qb, seq), I32), pltpu.VMEM((qb, seq), F32)],
        compiler_params=_cparams(("parallel", "arbitrary")), name="dsa_prompt")(q, qi, misc, kb, vb, kid)


PAGES_PER_STEP = 8
DSA_Q_HEADS = 8
DSA_KV_HEADS = 4


def _dsa_sample_score_kernel(pt_ref, qi_ref, w_ref, *refs):
    pages, out_ref = refs[:PAGES_PER_STEP], refs[PAGES_PER_STEP]
    qi = qi_ref[0]
    w = w_ref[0] * (IDX_HEADS ** -0.5)
    for j, pg in enumerate(pages):
        logit = _dot(qi, pg[...].astype(BF16))
        out_ref[0, j:j + 1, :] = jnp.sum(jnp.maximum(logit, 0.0) * w, axis=0, keepdims=True)


def _dsa_sample_scores(page_table, cache_kidx_t, li, qi3, w3):
    bd, n_pages = page_table.shape
    groups = n_pages // PAGES_PER_STEP

    def page_spec(j):
        return pl.BlockSpec((None, None, HEAD_DIM, PAGE_SIZE),
                            lambda b, g, pt: (li, pt[b, g * PAGES_PER_STEP + j], 0, 0))

    grid_spec = pltpu.PrefetchScalarGridSpec(
        num_scalar_prefetch=1, grid=(bd, groups),
        in_specs=[pl.BlockSpec((1, IDX_HEADS, HEAD_DIM), lambda b, g, pt: (b, 0, 0)),
                  pl.BlockSpec((1, IDX_HEADS, 1), lambda b, g, pt: (b, 0, 0))]
        + [page_spec(j) for j in range(PAGES_PER_STEP)],
        out_specs=pl.BlockSpec((1, PAGES_PER_STEP, PAGE_SIZE), lambda b, g, pt: (b, g, 0)))
    return pl.pallas_call(
        _dsa_sample_score_kernel, grid_spec=grid_spec,
        out_shape=jax.ShapeDtypeStruct((bd, n_pages, PAGE_SIZE), F32),
        compiler_params=_cparams(("parallel", "arbitrary")), name="dsa_sample_scores")(
            page_table, qi3, w3, *([cache_kidx_t] * PAGES_PER_STEP))


def _dsa_sample_threshold_kernel(sc_ref, qi_ref, ki_ref, w_ref, thr_ref, self_ref, key_ref, *, topk):
    qi = qi_ref[...].astype(F32)
    ki = ki_ref[...].astype(BF16).astype(F32)
    w = w_ref[...] * (IDX_HEADS ** -0.5)
    logit = jnp.sum(qi * ki, axis=2, keepdims=True)
    self_sc = jnp.sum(jnp.maximum(logit, 0.0) * w, axis=1)
    self_key = _sort_key(self_sc)
    key_ref[...] = _sort_key(sc_ref[...])

    def count_ge(thr):
        c = jnp.sum(jnp.where(key_ref[...] >= thr, 1.0, 0.0), axis=1, keepdims=True)
        return c + jnp.where(self_key >= thr, 1.0, 0.0)

    thr = _kth_largest_key(count_ge, self_key.shape, float(topk))
    thr_ref[...] = thr
    self_ref[...] = jnp.where(self_key >= thr, 1.0, 0.0)


def _dsa_sample_threshold(scores2, qi3, ki3, w3, topk):
    bd, past = scores2.shape
    kern = functools.partial(_dsa_sample_threshold_kernel, topk=topk)
    return pl.pallas_call(
        kern,
        out_shape=[jax.ShapeDtypeStruct((bd, 1), I32), jax.ShapeDtypeStruct((bd, 1), F32)],
        scratch_shapes=[pltpu.VMEM((bd, past), I32)],
        compiler_params=pltpu.CompilerParams(vmem_limit_bytes=VMEM_LIMIT),
        name="dsa_sample_threshold")(scores2, qi3, ki3, w3)


def _dsa_sample_attn_kernel(pt_ref, qbd_ref, sc_ref, thr_ref, self_ref, kn_ref, vn_ref, *refs,
                            n_groups):
    kpages = refs[:PAGES_PER_STEP]
    vpages = refs[PAGES_PER_STEP:2 * PAGES_PER_STEP]
    o_ref, m_ref, l_ref, acc_ref = refs[2 * PAGES_PER_STEP:]
    g = pl.program_id(1)
    nh = qbd_ref.shape[1]

    @pl.when(g == 0)
    def _():
        m_ref[...] = jnp.full(m_ref.shape, NEG_BIG, F32)
        l_ref[...] = jnp.zeros(l_ref.shape, F32)
        acc_ref[...] = jnp.zeros(acc_ref.shape, F32)

    qbd = qbd_ref[0]
    thr = thr_ref[0]

    for j in range(PAGES_PER_STEP):
        keep = _sort_key(sc_ref[0, j:j + 1, :]) >= thr
        sc = jnp.where(keep, _dot(qbd, kpages[j][...].astype(BF16)), NEG_BIG)
        m_old = m_ref[...]
        m_new = jnp.maximum(m_old, jnp.max(sc, axis=1, keepdims=True))
        alpha = jnp.exp(m_old - m_new)
        pe = jnp.where(keep, jnp.exp(sc - m_new), 0.0)
        l_ref[...] = alpha * l_ref[...] + jnp.sum(pe, axis=1, keepdims=True)
        acc_ref[...] = alpha * acc_ref[...] + _dot_nt(pe.astype(BF16), vpages[j][...].astype(BF16))
        m_ref[...] = m_new

    @pl.when(g == n_groups - 1)
    def _():
        kn = kn_ref[0].astype(BF16).astype(F32)
        sc = jnp.sum(qbd.astype(F32) * kn, axis=1, keepdims=True)
        sc = jnp.where(self_ref[0] > 0.5, sc, NEG_BIG)
        m_old = m_ref[...]
        m_new = jnp.maximum(m_old, sc)
        alpha = jnp.exp(m_old - m_new)
        pe = jnp.exp(sc - m_new)
        pe = jnp.where(sc > 0.5 * NEG_BIG, pe, 0.0)
        l = alpha * l_ref[...] + pe
        acc = alpha * acc_ref[...] + pe * vn_ref[0].astype(BF16).astype(F32)
        res = acc / l
        kvh = lax.broadcasted_iota(I32, (nh, HEAD_DIM), 0) // 2
        out = jnp.zeros((nh, HEAD_DIM), F32)
        for c in range(4):
            out = out + jnp.where(kvh == c, res[:, c * HEAD_DIM:(c + 1) * HEAD_DIM], 0.0)
        o_ref[0] = out


def _dsa_sample_attention(page_table, cache_k2, cache_v2, li, qbd, scores3, thr3, self3, kn3, vn3):
    bd, n_pages = page_table.shape
    groups = n_pages // PAGES_PER_STEP
    kvw = cache_k2.shape[-2]

    def page_spec(j):
        return pl.BlockSpec((None, None, kvw, PAGE_SIZE),
                            lambda b, g, pt: (li, pt[b, g * PAGES_PER_STEP + j], 0, 0))

    per_b = lambda shp: pl.BlockSpec((1,) + shp, lambda b, g, pt: (b, 0, 0))
    grid_spec = pltpu.PrefetchScalarGridSpec(
        num_scalar_prefetch=1, grid=(bd, groups),
        in_specs=[per_b((DSA_Q_HEADS, kvw)),
                  pl.BlockSpec((1, PAGES_PER_STEP, PAGE_SIZE), lambda b, g, pt: (b, g, 0)),
                  per_b((1, 1)), per_b((1, 1)), per_b((1, kvw)), per_b((1, kvw))]
        + [page_spec(j) for j in range(PAGES_PER_STEP)] * 2,
        out_specs=per_b((DSA_Q_HEADS, HEAD_DIM)),
        scratch_shapes=[pltpu.VMEM((DSA_Q_HEADS, 1), F32), pltpu.VMEM((DSA_Q_HEADS, 1), F32),
                        pltpu.VMEM((DSA_Q_HEADS, kvw), F32)])
    kern = functools.partial(_dsa_sample_attn_kernel, n_groups=groups)
    return pl.pallas_call(
        kern, grid_spec=grid_spec,
        out_shape=jax.ShapeDtypeStruct((bd, DSA_Q_HEADS, HEAD_DIM), F32),
        compiler_params=_cparams(("parallel", "arbitrary")), name="dsa_sample_attention")(
            page_table, qbd, scores3, thr3, self3, kn3, vn3,
            *([cache_k2] * PAGES_PER_STEP), *([cache_v2] * PAGES_PER_STEP))


def _post_mix_kernel(oa_ref, od_ref, x_ref, p_ref, woa_ref, wod_ref, g1_ref, b1_ref, wr_ref, br_ref,
                     wpg_ref, wpp_ref, cnt0_ref, x1t_ref, base_ref, route_ref, cnt_ref):
    tb = x_ref.shape[0]

    @pl.when(pl.program_id(0) == 0)
    def _():
        cnt_ref[...] = cnt0_ref[...]

    mixv = _dot(oa_ref[...].astype(BF16), woa_ref[...]) + _dot(od_ref[...].astype(BF16), wod_ref[...])
    x1 = _layer_norm_rows(DEEPNORM_ALPHA * x_ref[...] + mixv, g1_ref[...], b1_ref[...])
    x1b = x1.astype(BF16)
    for j in range(SUBLANES):
        x1t_ref[pl.ds(j, tb, stride=SUBLANES), :] = x1[:, j * LANES:(j + 1) * LANES]
    ple = _sigmoid(_dot(x1b, wpg_ref[...])) * _dot(p_ref[...].astype(BF16), wpp_ref[...])
    base_ref[...] = DEEPNORM_ALPHA * x1 + ple

    logits = _dot(x1, wr_ref[...], precision=lax.Precision.HIGHEST) + br_ref[...]
    lane = lax.broadcasted_iota(I32, (tb, LANES), 1)
    lanef = lane.astype(F32)
    vals, hots = [], []
    work = logits
    for _ in range(TOP_K):
        m = jnp.max(work, axis=1, keepdims=True)
        idx = jnp.min(jnp.where(work == m, lanef, float(LANES)), axis=1, keepdims=True)
        hot = lanef == idx
        vals.append(m)
        hots.append(hot)
        work = jnp.where(hot, -jnp.inf, work)
    exps = [jnp.exp(v - vals[0]) for v in vals]
    den = exps[0] + exps[1] + exps[2] + exps[3]
    oh = jnp.zeros((tb, LANES), F32)
    for hot in hots:
        oh = oh + jnp.where(hot, 1.0, 0.0)
    r = lax.broadcasted_iota(I32, (tb, tb), 0)
    c = lax.broadcasted_iota(I32, (tb, tb), 1)
    strict = jnp.where(c < r, 1.0, 0.0).astype(BF16)
    before = _dot(strict, oh.astype(BF16)) + cnt_ref[...]
    route = jnp.zeros((tb, LANES), F32)
    for k in range(TOP_K):
        gate = exps[k] / den
        eidx = jnp.sum(jnp.where(hots[k], lanef, 0.0), axis=1, keepdims=True)
        rank = jnp.sum(jnp.where(hots[k], before, 0.0), axis=1, keepdims=True)
        route = route + jnp.where(lane == k, gate, 0.0) + jnp.where(lane == TOP_K + k, eidx, 0.0) \
            + jnp.where(lane == 2 * TOP_K + k, rank, 0.0)
    route_ref[...] = route
    cnt_ref[...] = cnt_ref[...] + jnp.sum(oh, axis=0, keepdims=True)


def _post_mix(oa, od, x, p, woa, wod, g1, b1, wr, br, wpg, wpp, cnt0, tb):
    n = x.shape[0]
    row = lambda w: pl.BlockSpec((tb, w), lambda i: (i, 0))
    const = lambda a: pl.BlockSpec(a.shape, lambda i: tuple(0 for _ in a.shape))
    return pl.pallas_call(
        _post_mix_kernel, grid=(n // tb,),
        in_specs=[row(512), row(512), row(D_MODEL), row(p.shape[1]), const(woa), const(wod), const(g1),
                  const(b1), const(wr), const(br), const(wpg), const(wpp), const(cnt0)],
        out_specs=[pl.BlockSpec((tb * SUBLANES, LANES), lambda i: (i, 0)), row(D_MODEL), row(LANES),
                   pl.BlockSpec((1, LANES), lambda i: (0, 0))],
        out_shape=[jax.ShapeDtypeStruct((n * SUBLANES, LANES), F32), jax.ShapeDtypeStruct((n, D_MODEL), F32),
                   jax.ShapeDtypeStruct((n, LANES), F32), jax.ShapeDtypeStruct((1, LANES), F32)],
        compiler_params=_cparams(("arbitrary",)), name="post_mix_router")(
            oa, od, x, p, woa, wod, g1, b1, wr, br, wpg, wpp, cnt0)


def _slot_table_kernel(dest_ref, tab_ref):
    unroll = 16

    def clear(c, carry):
        for u in range(unroll):
            tab_ref[c * unroll + u] = jnp.int32(-1)
        return carry

    lax.fori_loop(0, tab_ref.shape[0] // unroll, clear, 0)

    def place(c, carry):
        for u in range(unroll):
            a = c * unroll + u
            tab_ref[dest_ref[a]] = a
        return carry

    lax.fori_loop(0, dest_ref.shape[0] // unroll, place, 0)


def _slot_table(dest, n_blocks):
    return pl.pallas_call(
        _slot_table_kernel,
        in_specs=[pl.BlockSpec(memory_space=pltpu.SMEM)],
        out_specs=pl.BlockSpec(memory_space=pltpu.SMEM),
        out_shape=jax.ShapeDtypeStruct((n_blocks * MOE_BLK,), I32),
        name="moe_slot_table")(dest.reshape(-1))


def _fused_row_copy(src, src_row, dst, dst_row, sem):
    return pltpu.make_async_copy(src.at[pl.ds(src_row * SUBLANES, SUBLANES), :],
                                 dst.at[pl.ds(dst_row * SUBLANES, SUBLANES), :], sem)


def _moe_fused_kernel(be_ref, br_ref, nu_ref, tab_cur, tab_nxt, x_hbm, wg_ref, bg_ref, wu_ref, bu_ref, wd_ref,
                      bd_ref, y_hbm, xbuf, ybuf, gsem, ssem, wgb, wub, wdb, *, n_tok):
    i = pl.program_id(0)
    n_used = nu_ref[0]
    slot = i % 2

    def for_real_slots(n_real, fn):
        def group(c, carry):
            for u in range(DMA_UNROLL):
                fn(c * DMA_UNROLL + u)
            return carry

        n_full = lax.shift_right_logical(n_real, DMA_UNROLL_LOG2)
        lax.fori_loop(0, n_full, group, 0)

        def single(s, carry):
            fn(s)
            return carry

        lax.fori_loop(n_full * DMA_UNROLL, n_real, single, 0)

    def wait_rows(sem, n_real):
        for_real_slots(n_real, lambda s: _fused_row_copy(x_hbm, 0, xbuf.at[0], 0, sem).wait())

    def gather(tab, dst_slot, n_real):
        def one(s):
            tok = lax.shift_right_logical(tab[0, 0, s], TOP_K_LOG2)
            _fused_row_copy(x_hbm, tok, xbuf.at[dst_slot], s, gsem.at[dst_slot]).start()

        for_real_slots(n_real, one)

    def scatter(tab, src_slot, n_real):
        def one(s):
            a = tab[0, 0, s]
            row = (a & (TOP_K - 1)) * n_tok + lax.shift_right_logical(a, TOP_K_LOG2)
            _fused_row_copy(ybuf.at[src_slot], s, y_hbm, row, ssem.at[src_slot]).start()

        for_real_slots(n_real, one)

    real = lambda blk: br_ref[jnp.clip(blk, 0, n_used - 1)]

    @pl.when(i < n_used)
    def _():
        @pl.when(i == 0)
        def _():
            xbuf[...] = jnp.zeros(xbuf.shape, F32)
            gather(tab_cur, 0, real(0))

        @pl.when(i + 1 < n_used)
        def _():
            gather(tab_nxt, 1 - slot, real(i + 1))

        wait_rows(gsem.at[slot], real(i))

        @pl.when(i >= 2)
        def _():
            wait_rows(ssem.at[slot], real(i - 2))

        prev = be_ref[jnp.maximum(i - 1, 0)]

        @pl.when((i == 0) | (prev != be_ref[i]))
        def _():
            wgb[...] = wg_ref[...].astype(BF16)
            wub[...] = wu_ref[...].astype(BF16)
            wdb[...] = wd_ref[...].astype(BF16)

        x = jnp.concatenate([xbuf[slot, pl.ds(j, MOE_BLK, stride=SUBLANES), :] for j in range(SUBLANES)],
                            axis=1).astype(BF16)
        g = jnp.minimum(_dot(x, wgb[...]) + bg_ref[...], SWIGLU_LIMIT)
        u = jnp.clip(_dot(x, wub[...]) + bu_ref[...], -SWIGLU_LIMIT, SWIGLU_LIMIT)
        hdn = (u + 1.0) * g * _sigmoid(SWIGLU_ALPHA * g)
        y = _dot(hdn.astype(BF16), wdb[...]) + bd_ref[...]
        for j in range(SUBLANES):
            ybuf[slot, pl.ds(j, MOE_BLK, stride=SUBLANES), :] = y[:, j * LANES:(j + 1) * LANES]
        scatter(tab_cur, slot, real(i))

        @pl.when(i == n_used - 1)
        def _():
            wait_rows(ssem.at[slot], real(i))

            @pl.when(i >= 1)
            def _():
                wait_rows(ssem.at[1 - slot], real(i - 1))


def _moe_fused(block_expert, block_real, n_used, tab3, x1t, li, wg, bg, wu, bu, wd, bd):
    n_blocks = block_expert.shape[0]
    n_tok = x1t.shape[0] // SUBLANES
    d, f = wg.shape[2], wg.shape[3]

    def w_idx(i, be, br, nu):
        return (li, be[jnp.minimum(i, nu[0] - 1)], 0, 0)

    tab_spec = lambda step: pl.BlockSpec(
        (1, 1, MOE_BLK), lambda i, be, br, nu: (jnp.minimum(i + step, nu[0] - 1), 0, 0),
        memory_space=pltpu.SMEM)
    grid_spec = pltpu.PrefetchScalarGridSpec(
        num_scalar_prefetch=3, grid=(n_blocks,),
        in_specs=[tab_spec(0), tab_spec(1), pl.BlockSpec(memory_space=pl.ANY),
                  pl.BlockSpec((None, None, d, f), w_idx), pl.BlockSpec((None, None, 1, f), w_idx),
                  pl.BlockSpec((None, None, d, f), w_idx), pl.BlockSpec((None, None, 1, f), w_idx),
                  pl.BlockSpec((None, None, f, d), w_idx), pl.BlockSpec((None, None, 1, d), w_idx)],
        out_specs=pl.BlockSpec(memory_space=pl.ANY),
        scratch_shapes=[pltpu.VMEM((2, MOE_BLK * SUBLANES, LANES), F32),
                        pltpu.VMEM((2, MOE_BLK * SUBLANES, LANES), F32),
                        pltpu.SemaphoreType.DMA((2,)), pltpu.SemaphoreType.DMA((2,)),
                        pltpu.VMEM((d, f), BF16), pltpu.VMEM((d, f), BF16), pltpu.VMEM((f, d), BF16)])
    kern = functools.partial(_moe_fused_kernel, n_tok=n_tok)
    return pl.pallas_call(
        kern, grid_spec=grid_spec,
        out_shape=jax.ShapeDtypeStruct((TOP_K * n_tok * SUBLANES, LANES), F32),
        compiler_params=_cparams(("arbitrary",)), name="moe_experts")(
            block_expert, block_real, n_used, tab3, tab3, x1t, wg, bg, wu, bu, wd, bd)


def _moe_sum_kernel(y0_ref, y1_ref, y2_ref, y3_ref, route_ref, base_ref, g2_ref, b2_ref, o_ref):
    tb = base_ref.shape[0]
    gates = route_ref[...][:, 0:TOP_K]
    cols = []
    for j in range(SUBLANES):
        acc = jnp.zeros((tb, LANES), F32)
        for k, y_ref in enumerate((y0_ref, y1_ref, y2_ref, y3_ref)):
            acc = acc + gates[:, k:k + 1] * y_ref[pl.ds(j, tb, stride=SUBLANES), :]
        cols.append(acc)
    ffn = jnp.concatenate(cols, axis=1)
    o_ref[...] = _layer_norm_rows(base_ref[...] + ffn, g2_ref[...], b2_ref[...])


def _moe_sum(y_tok, route, base, g2, b2, tb):
    n = base.shape[0]
    steps = n // tb
    row = lambda w: pl.BlockSpec((tb, w), lambda i: (i, 0))
    const = lambda a: pl.BlockSpec(a.shape, lambda i: tuple(0 for _ in a.shape))
    y_spec = lambda k: pl.BlockSpec((tb * SUBLANES, LANES), lambda i: (k * steps + i, 0))
    return pl.pallas_call(
        _moe_sum_kernel, grid=(steps,),
        in_specs=[y_spec(k) for k in range(TOP_K)] + [row(LANES), row(D_MODEL), const(g2), const(b2)],
        out_specs=row(D_MODEL),
        out_shape=jax.ShapeDtypeStruct((n, D_MODEL), F32),
        compiler_params=_cparams(("parallel",)), name="moe_combine")(
            *([y_tok] * TOP_K), route, base, g2, b2)


def _moe_layer(x1t, route, cnt, base, lw, ew, li, tb):
    n = base.shape[0]
    a = n * TOP_K
    n_blocks = a // MOE_BLK + N_EXPERTS
    counts = cnt[0, :N_EXPERTS].astype(I32)
    padded = (counts + MOE_BLK - 1) // MOE_BLK * MOE_BLK
    pad_end = jnp.cumsum(padded)
    pad_start = pad_end - padded
    top_idx = route[:, TOP_K:2 * TOP_K].astype(I32)
    rank = route[:, 2 * TOP_K:3 * TOP_K].astype(I32)
    dest = pad_start[top_idx] + rank
    block_start = jnp.arange(n_blocks, dtype=I32) * MOE_BLK
    block_expert = jnp.minimum(jnp.sum((pad_end[None, :] <= block_start[:, None]).astype(I32), axis=1),
                               N_EXPERTS - 1).astype(I32)
    n_used = (pad_end[-1:] // MOE_BLK).astype(I32)
    tab3 = _slot_table(dest, n_blocks).reshape(n_blocks, 1, MOE_BLK)
    block_real = jnp.clip(counts[block_expert] - (block_start - pad_start[block_expert]), 0, MOE_BLK)
    y_tok = _moe_fused(block_expert, block_real.astype(I32), n_used, tab3, x1t, li, *ew)
    return _moe_sum(y_tok, route, base, lw['ln2_g'], lw['ln2_b'], tb)


_DQ_PROMPT_ORDER = (0, 2, 1, 3, 4, 6, 5, 7)


def _relayout_w_in(w, head_order):
    col = lambda name, width: w[:, _OFF[name]:_OFF[name] + width]
    dq = col('dq', 512).reshape(D_MODEL, 8, HEAD_DIM)[:, jnp.array(head_order), :].reshape(D_MODEL, 512)
    z = lambda width: jnp.zeros((D_MODEL, width), w.dtype)
    misc = jnp.concatenate([col('ik', 64), z(16), col('iw', 8), col('glr', 16), z(24)], axis=1)
    out = jnp.concatenate([col('gq', 256), col('gk', 256), col('gv', 256), col('gr', 256),
                           col('hq', 256), col('hf', 256), col('hi', 256), col('hgt', 256),
                           dq, col('dk', 256), col('dv', 256), col('iq', 512), misc], axis=1)
    return out.astype(BF16)


def _rope_tables(pos):
    half = ROPE_DIM // 2
    inv = ROPE_THETA ** (-jnp.arange(half, dtype=F32) / half)
    ang = pos.astype(F32)[:, None] * inv[None, :]
    cos, sin = jnp.cos(ang), jnp.sin(ang)
    t = pos.shape[0]
    one, zero = jnp.ones((t, HEAD_DIM - ROPE_DIM), F32), jnp.zeros((t, HEAD_DIM - ROPE_DIM), F32)
    z8 = jnp.zeros((t, half), F32)
    c64 = jnp.concatenate([cos, cos, one], axis=1)
    a64 = jnp.concatenate([-sin, z8, zero], axis=1)
    b64 = jnp.concatenate([z8, sin, zero], axis=1)
    tile2 = lambda a: jnp.concatenate([a, a], axis=1)
    return tile2(c64), tile2(a64), tile2(b64)


def _layer_weights(li, w):
    wo = w['w_out'][li]
    wod = wo[512:].reshape(8, HEAD_DIM, D_MODEL)
    wg_pad = jnp.zeros((LANES, 256), F32).at[MISC_GLR:MISC_GLR + 16].set(w['w_gla_gate'][li])
    wr = jnp.zeros((D_MODEL, LANES), F32).at[:, :N_EXPERTS].set(w['w_router'][li])
    br = jnp.full((1, LANES), NEG_BIG, F32).at[0, :N_EXPERTS].set(w['b_router'][li])
    tile2 = lambda g: jnp.concatenate([g, g]).reshape(1, LANES)
    return dict(
        w_in_p=_relayout_w_in(w['w_in'][li], _DQ_PROMPT_ORDER),
        wg_pad=wg_pad, bg=w['b_gla_gate'][li].reshape(1, 256),
        gn=tile2(w['gla_norm_g'][li]), hn=tile2(w['hg_norm_g'][li]),
        woa=wo[:512].astype(BF16),
        wod_p=wod[jnp.array(_DQ_PROMPT_ORDER)].reshape(512, D_MODEL).astype(BF16),
        ln1_g=w['ln1_g'][li].reshape(1, D_MODEL), ln1_b=w['ln1_b'][li].reshape(1, D_MODEL),
        wr=wr, br=br,
        wpg=w['w_ple_gate'][li].astype(BF16), wpp=w['w_ple_proj'][li].astype(BF16),
        ln2_g=w['ln2_g'][li].reshape(1, D_MODEL), ln2_b=w['ln2_b'][li].reshape(1, D_MODEL))


def _state_to_blockdiag(s):
    b = s.shape[0]
    st = jnp.swapaxes(s, -1, -2).reshape(b, 2, 2, HEAD_DIM, HEAD_DIM)
    z = jnp.zeros_like(st[:, :, 0])
    top = jnp.concatenate([st[:, :, 0], z], axis=-1)
    bot = jnp.concatenate([z, st[:, :, 1]], axis=-1)
    return jnp.concatenate([top, bot], axis=-2)


def _state_from_blockdiag(sbd):
    b = sbd.shape[0]
    h0 = sbd[:, :, :HEAD_DIM, :HEAD_DIM]
    h1 = sbd[:, :, HEAD_DIM:, HEAD_DIM:]
    st = jnp.stack([h0, h1], axis=2).reshape(b, 4, HEAD_DIM, HEAD_DIM)
    return jnp.swapaxes(st, -1, -2)


def kernel(x_prompt, x_sample, cache_k, cache_v, cache_kidx, state_gla, state_hgrn, page_table, p_prompt, p_sample, ln_in_g, ln_in_b, w_in, w_gla_gate, b_gla_gate, gla_norm_g, hg_gamma, hg_norm_g, w_out, ln1_g, ln1_b, w_router, b_router, w_exp_gate, b_exp_gate, w_exp_up, b_exp_up, w_exp_down, b_exp_down, ln2_g, ln2_b, w_ple_gate, w_ple_proj):
    bp, t, d = x_prompt.shape
    bs, ts, _ = x_sample.shape
    assert ts == 1 and d == D_MODEL
    depth = w_in.shape[0]
    n_pages = page_table.shape[1]
    past = n_pages * PAGE_SIZE
    n_p = bp * t
    weights = dict(w_in=w_in, w_gla_gate=w_gla_gate, b_gla_gate=b_gla_gate, gla_norm_g=gla_norm_g,
                   hg_norm_g=hg_norm_g, w_out=w_out, ln1_g=ln1_g, ln1_b=ln1_b, w_router=w_router,
                   b_router=b_router, w_exp_gate=w_exp_gate, b_exp_gate=b_exp_gate, w_exp_up=w_exp_up,
                   b_exp_up=b_exp_up, w_exp_down=w_exp_down, b_exp_down=b_exp_down, ln2_g=ln2_g,
                   ln2_b=ln2_b, w_ple_gate=w_ple_gate, w_ple_proj=w_ple_proj)

    tm_p = min(512, t)
    tb_p = min(256, t)
    lbs = _lower_bounds(hg_gamma)
    tabs_p = _rope_tables(jnp.arange(t, dtype=I32))
    tabs_s = _rope_tables(jnp.full((bs,), past, I32))
    xp = _input_layer_norm(x_prompt.reshape(n_p, d), ln_in_g, ln_in_b, tm_p)
    xs = _input_layer_norm(x_sample.reshape(bs, d), ln_in_g, ln_in_b, bs)
    pages_t = lambda c: jnp.transpose(c, (0, 1, 3, 4, 2)).reshape(c.shape[0], c.shape[1], -1, PAGE_SIZE)
    cache_k2, cache_v2 = pages_t(cache_k), pages_t(cache_v)
    cache_kidx_t = jnp.transpose(cache_kidx, (0, 1, 3, 2))
    bias4 = lambda b: b.reshape(b.shape[0], b.shape[1], 1, b.shape[2])
    ew = (w_exp_gate, bias4(b_exp_gate), w_exp_up, bias4(b_exp_up), w_exp_down, bias4(b_exp_down))
    head_perm = jnp.array(_DQ_PROMPT_ORDER)
    zero_state = jnp.zeros((bp, 4, LANES, LANES), F32)
    zero_cnt = jnp.zeros((1, LANES), F32)
    topk_s = min(TOPK_MAX, (past + ts) // 4)
    head_of_row = jnp.arange(DSA_Q_HEADS)[:, None] // 2
    kv_of_lane = jnp.arange(DSA_KV_HEADS * HEAD_DIM)[None, :] // HEAD_DIM

    outs = {k: [] for k in ('kp', 'vp', 'kip', 'sgp', 'shp', 'ks', 'vs', 'kis', 'sgs', 'shs')}
    for li in range(depth):
        lw = _layer_weights(li, weights)
        lb = lbs[li].reshape(1, 256)

        (g_all, h_all, misc, q, qi, k, v, ki, kb, vb, kid) = _in_projection(xp, lw['w_in_p'], *tabs_p, tm_p)
        o_lin, st = _gla_hgrn(g_all, h_all, misc, lw['wg_pad'], lw['bg'], lb, lw['gn'], lw['hn'],
                              zero_state, bp, t, t)
        o_dsa = _dsa_prompt(q, qi, misc, kb, vb, kid, bp, t)
        x1t, base, route, cnt = _post_mix(o_lin, o_dsa, xp, p_prompt[li].reshape(n_p, -1), lw['woa'],
                                          lw['wod_p'], lw['ln1_g'], lw['ln1_b'], lw['wr'], lw['br'],
                                          lw['wpg'], lw['wpp'], zero_cnt, tb_p)
        xp = _moe_layer(x1t, route, cnt, base, lw, ew, li, tb_p)
        outs['kp'].append(k.reshape(bp, t, DSA_KV_HEADS, HEAD_DIM))
        outs['vp'].append(v.reshape(bp, t, DSA_KV_HEADS, HEAD_DIM))
        outs['kip'].append(ki.reshape(bp, t, HEAD_DIM))
        outs['sgp'].append(_state_from_blockdiag(st[:, 0:2]))
        outs['shp'].append(_state_from_blockdiag(st[:, 2:4]))

        (g_all, h_all, misc, q, qi, k, v, ki, kb, vb, kid) = _in_projection(xs, lw['w_in_p'], *tabs_s, bs)
        pad = lambda a: jnp.pad(a.reshape(bs, 1, -1), ((0, 0), (0, GLA_CHUNK - 1), (0, 0))).reshape(
            bs * GLA_CHUNK, -1)
        s0 = jnp.concatenate([_state_to_blockdiag(state_gla[li]), _state_to_blockdiag(state_hgrn[li])], axis=1)
        o_lin, st = _gla_hgrn(pad(g_all), pad(h_all), pad(misc), lw['wg_pad'], lw['bg'], lb, lw['gn'],
                              lw['hn'], s0, bs, GLA_CHUNK, 1)
        o_lin = o_lin.reshape(bs, GLA_CHUNK, -1)[:, 0]
        qi3 = qi.reshape(bs, IDX_HEADS, HEAD_DIM)
        w3 = misc[:, MISC_IW:MISC_IW + IDX_HEADS].reshape(bs, IDX_HEADS, 1)
        scores = _dsa_sample_scores(page_table, cache_kidx_t, li, qi3, w3)
        thr, self_sel = _dsa_sample_threshold(scores.reshape(bs, past), qi3, ki.reshape(bs, 1, HEAD_DIM),
                                              w3, topk_s)
        q3 = q.reshape(bs, DSA_Q_HEADS, HEAD_DIM)[:, head_perm].reshape(bs, DSA_Q_HEADS, 1, HEAD_DIM)
        qbd = jnp.where(head_of_row == kv_of_lane,
                        jnp.tile(q3, (1, 1, DSA_KV_HEADS, 1)).reshape(bs, DSA_Q_HEADS, -1), 0).astype(BF16)
        o_dsa = _dsa_sample_attention(page_table, cache_k2, cache_v2, li, qbd, scores,
                                      thr.reshape(bs, 1, 1), self_sel.reshape(bs, 1, 1),
                                      k.reshape(bs, 1, -1), v.reshape(bs, 1, -1))
        o_dsa = o_dsa[:, head_perm].reshape(bs, 512)
        x1t, base, route, cnt = _post_mix(o_lin, o_dsa, xs, p_sample[li].reshape(bs, -1), lw['woa'],
                                          lw['wod_p'], lw['ln1_g'], lw['ln1_b'], lw['wr'], lw['br'],
                                          lw['wpg'], lw['wpp'], zero_cnt, bs)
        xs = _moe_layer(x1t, route, cnt, base, lw, ew, li, bs)
        outs['ks'].append(k.reshape(bs, ts, DSA_KV_HEADS, HEAD_DIM))
        outs['vs'].append(v.reshape(bs, ts, DSA_KV_HEADS, HEAD_DIM))
        outs['kis'].append(ki.reshape(bs, ts, HEAD_DIM))
        outs['sgs'].append(_state_from_blockdiag(st[:, 0:2]))
        outs['shs'].append(_state_from_blockdiag(st[:, 2:4]))

    stack = lambda name: jnp.stack(outs[name], 0)
    return (xp.reshape(bp, t, d), xs.reshape(bs, ts, d),
            stack('kp'), stack('vp'), stack('kip'), stack('sgp'), stack('shp'),
            stack('ks'), stack('vs'), stack('kis'), stack('sgs'), stack('shs'))
```

```python
import functools

import jax
import jax.numpy as jnp
from jax import lax
from jax.experimental import pallas as pl
from jax.experimental.pallas import tpu as pltpu

F32 = jnp.float32
BF16 = jnp.bfloat16
I32 = jnp.int32

LANES = 128
SUBLANES = 8
VMEM_LIMIT = 56 * 1024 * 1024

D_MODEL = 1024
HEAD_DIM = 64
PAGE_SIZE = 128
IDX_HEADS = 8
N_EXPERTS = 32
TOP_K = 4
TOPK_MAX = 256
ROPE_DIM = 16
ROPE_THETA = 500000.0
GLA_GATE_TAU = 16.0
SWIGLU_ALPHA = 1.702
SWIGLU_LIMIT = 7.0
LN_EPS = 1e-5
DEPTH = 4
DEEPNORM_ALPHA = (2 * DEPTH) ** 0.25

_OFF = dict(gq=0, gk=256, gv=512, gr=768, glr=1024, hq=1040, hf=1296, hi=1552, hgt=1808,
            dq=2064, dk=2576, dv=2832, iq=3088, iw=3600, ik=3608)
PROJ_W = 3712
MISC_IW = 80
MISC_GLR = 88
NEG_BIG = -1e30

GLA_CHUNK = 128
GLA_SUB = 16
GLA_EXP_CLAMP = 60.0

MOE_BLK = 256
MOE_BLK_LOG2 = 8
TOP_K_LOG2 = 2
assert MOE_BLK == 1 << MOE_BLK_LOG2 and TOP_K == 1 << TOP_K_LOG2
DMA_UNROLL = 8
DMA_UNROLL_LOG2 = 3
assert DMA_UNROLL == 1 << DMA_UNROLL_LOG2


def _cparams(sem, vmem=VMEM_LIMIT):
    return pltpu.CompilerParams(dimension_semantics=sem, vmem_limit_bytes=vmem)


def _dot(a, b, precision=None):
    return jnp.dot(a, b, preferred_element_type=F32, precision=precision)


def _dot_nt(a, b, precision=None):
    return lax.dot_general(a, b, (((1,), (1,)), ((), ())), preferred_element_type=F32,
                           precision=precision)


def _sigmoid(x):
    return 1.0 / (1.0 + jnp.exp(-x))


def _log_sigmoid(x):
    return jnp.minimum(x, 0.0) - jnp.log(1.0 + jnp.exp(-jnp.abs(x)))


def _layer_norm_rows(x, g, b):
    mu = jnp.mean(x, axis=-1, keepdims=True)
    xc = x - mu
    var = jnp.mean(xc * xc, axis=-1, keepdims=True)
    return xc * lax.rsqrt(var + LN_EPS) * g + b


def _lower_bound_kernel(gam_ref, lb_ref):
    g = gam_ref[...]
    m = jnp.max(g, axis=0, keepdims=True)
    e = jnp.exp(g - m)
    sm = e / jnp.sum(e, axis=0, keepdims=True)
    acc = jnp.zeros_like(sm[0:1])
    for li in range(g.shape[0]):
        acc = acc + sm[li:li + 1]
        lb_ref[li:li + 1, :] = jnp.maximum(acc - sm[0:1], 0.0)


def _lower_bounds(hg_gamma):
    return pl.pallas_call(
        _lower_bound_kernel, out_shape=jax.ShapeDtypeStruct(hg_gamma.shape, F32),
        name="hgrn_lower_bounds")(hg_gamma.astype(F32))


def _ln_kernel(x_ref, g_ref, b_ref, o_ref):
    o_ref[...] = _layer_norm_rows(x_ref[...], g_ref[...], b_ref[...])


def _input_layer_norm(x, g, b, tm):
    n, d = x.shape
    return pl.pallas_call(
        _ln_kernel, grid=(n // tm,),
        in_specs=[pl.BlockSpec((tm, d), lambda i: (i, 0)),
                  pl.BlockSpec((1, d), lambda i: (0, 0)),
                  pl.BlockSpec((1, d), lambda i: (0, 0))],
        out_specs=pl.BlockSpec((tm, d), lambda i: (i, 0)),
        out_shape=jax.ShapeDtypeStruct((n, d), F32),
        compiler_params=_cparams(("parallel",)), name="input_ln")(x, g.reshape(1, d), b.reshape(1, d))


def _inproj_kernel(h_ref, w_ref, cos_ref, sa_ref, sb_ref,
                   g_ref, hh_ref, misc_ref, q_ref, qi_ref, k_ref, v_ref, ki_ref,
                   kb_ref, vb_ref, kid_ref):
    hb = h_ref[...].astype(BF16)
    cos, sa, sb = cos_ref[...], sa_ref[...], sb_ref[...]

    def proj(lo, width):
        return _dot(hb, w_ref[:, lo:lo + width])

    def rope(x):
        return x * cos + pltpu.roll(x, LANES - 8, 1) * sa + pltpu.roll(x, 8, 1) * sb

    g_ref[...] = proj(0, 1024)
    hh_ref[...] = proj(1024, 1024)
    for s in range(4):
        q_ref[:, s * LANES:(s + 1) * LANES] = (rope(proj(2048 + s * LANES, LANES)) * 0.125).astype(BF16)
        qi_ref[:, s * LANES:(s + 1) * LANES] = (rope(proj(3072 + s * LANES, LANES)) * 0.125).astype(BF16)
    for s in range(2):
        kr = rope(proj(2560 + s * LANES, LANES))
        k_ref[:, s * LANES:(s + 1) * LANES] = kr
        kb_ref[:, s * LANES:(s + 1) * LANES] = kr.astype(BF16)
    vv = proj(2816, 256)
    v_ref[...] = vv
    vb_ref[...] = vv.astype(BF16)
    misc = rope(proj(3584, LANES))
    misc_ref[...] = misc
    ki_ref[...] = misc[:, 0:HEAD_DIM]
    lane = lax.broadcasted_iota(I32, misc.shape, 1)
    ki_lo = jnp.where(lane < HEAD_DIM, misc, 0.0)
    kid_ref[...] = (ki_lo + pltpu.roll(ki_lo, HEAD_DIM, 1)).astype(BF16)


def _in_projection(h, w_r, cos, sa, sb, tm):
    n = h.shape[0]
    t_blocks = cos.shape[0] // tm
    row = lambda w: pl.BlockSpec((tm, w), lambda i: (i, 0))
    tab = pl.BlockSpec((tm, LANES), lambda i: (i % t_blocks, 0))
    widths = [(1024, F32), (1024, F32), (LANES, F32), (512, BF16), (512, BF16), (256, F32), (256, F32),
              (HEAD_DIM, F32), (256, BF16), (256, BF16), (LANES, BF16)]
    return pl.pallas_call(
        _inproj_kernel, grid=(n // tm,),
        in_specs=[row(D_MODEL), pl.BlockSpec((D_MODEL, PROJ_W), lambda i: (0, 0)), tab, tab, tab],
        out_specs=[row(w) for w, _ in widths],
        out_shape=[jax.ShapeDtypeStruct((n, w), dt) for w, dt in widths],
        compiler_params=_cparams(("parallel",)), name="in_projection")(h, w_r, cos, sa, sb)


def _cumsum_rows(x):
    n = x.shape[0]
    row = lax.broadcasted_iota(I32, x.shape, 0)
    sh = 1
    while sh < n:
        x = x + jnp.where(row >= sh, pltpu.roll(x, sh, 0), 0.0)
        sh *= 2
    return x


def _gla_kernel(g_ref, h_ref, misc_ref, wg_ref, bg_ref, lb_ref, gn_ref, hn_ref, s0_ref,
                o_ref, sT_ref, state, *, n_chunks, t_valid):
    c_idx = pl.program_id(1)

    @pl.when(c_idx == 0)
    def _():
        state[...] = s0_ref[0]

    ch = GLA_CHUNK
    n_sub = ch // GLA_SUB
    row = lax.broadcasted_iota(I32, (ch, ch), 0)
    col = lax.broadcasted_iota(I32, (ch, ch), 1)
    bd = jnp.where((row < HEAD_DIM) == (col < HEAD_DIM), 1.0, 0.0).astype(F32)
    lane = lax.broadcasted_iota(I32, (ch, LANES), 1)
    lane32 = lax.broadcasted_iota(I32, (2 * GLA_SUB, LANES), 1)
    row32 = lax.broadcasted_iota(I32, (2 * GLA_SUB, LANES), 0)
    head_sel = jnp.where((row32 < GLA_SUB) == (lane32 < HEAD_DIM), 1.0, 0.0).astype(F32)
    lane16 = lax.broadcasted_iota(I32, (GLA_SUB, LANES), 1)

    def chunk(ci, carry):
        r0 = pl.multiple_of(ci * ch, ch)
        rows = pl.ds(r0, ch)
        gx = g_ref[rows, :]
        hx = h_ref[rows, :]
        z = _dot(misc_ref[rows, :], wg_ref[...], precision=lax.Precision.HIGHEST) + bg_ref[...]
        lg_gla = _log_sigmoid(z) * (1.0 / GLA_GATE_TAU)
        lb = lb_ref[...]
        zf = hx[:, 256:512]
        a_ = jnp.log(lb)
        c_ = jnp.log(1.0 - lb) + _log_sigmoid(zf)
        lg_hg = jnp.maximum(a_, c_) + jnp.log(1.0 + jnp.exp(-jnp.abs(a_ - c_)))
        k_hg = (1.0 - lb) * _sigmoid(-zf)
        k_gla = gx[:, 256:512]
        if t_valid is not None:
            tpos = c_idx * (n_chunks * ch) + r0 + lax.broadcasted_iota(I32, (ch, 1), 0)
            live = tpos < t_valid
            lg_gla, lg_hg = jnp.where(live, lg_gla, 0.0), jnp.where(live, lg_hg, 0.0)
            k_gla, k_hg = jnp.where(live, k_gla, 0.0), jnp.where(live, k_hg, 0.0)
        hq = hx[:, 0:256]
        gr = gx[:, 768:1024]
        mix = ((gx[:, 0:256] * (HEAD_DIM ** -0.5), k_gla, gx[:, 512:768], _cumsum_rows(lg_gla),
                gr * _sigmoid(gr), gn_ref[...]),
               (hq * _sigmoid(hq), k_hg, hx[:, 512:768], _cumsum_rows(lg_hg),
                _sigmoid(hx[:, 768:1024]), hn_ref[...]))
        pairs = [(mi, p) for mi in range(2) for p in range(2)]
        sl = lambda p: slice(p * LANES, (p + 1) * LANES)

        prep = []
        for mi, p in pairs:
            qm, km, vm, bm = (mix[mi][j][:, sl(p)] for j in range(4))
            b_last = bm[ch - 1:ch, :]
            lhs, kss = [], []
            for i in range(n_sub):
                lo, n = i * GLA_SUB, (i + 1) * GLA_SUB
                ref = bm[lo - 1:lo, :] if i > 0 else jnp.zeros((1, LANES), F32)
                qs = qm[lo:n, :] * jnp.exp(bm[lo:n, :] - ref)
                lhs.append((jnp.concatenate([qs, qs], axis=0) * head_sel).astype(BF16))
                kss.append((km[0:n, :] * jnp.exp(jnp.minimum(ref - bm[0:n, :], GLA_EXP_CLAMP))).astype(BF16))
            prep.append(dict(qd=(qm * jnp.exp(bm)).astype(BF16), vb=vm.astype(BF16), vt=vm.T.astype(BF16),
                             kd=(km * jnp.exp(b_last - bm)).astype(BF16), decay=jnp.exp(b_last),
                             lhs=lhs, kss=kss))

        outs = []
        for idx, (mi, p) in enumerate(pairs):
            pr = prep[idx]
            sT = state[2 * mi + p]
            o_inter = _dot_nt(pr['qd'], sT.astype(BF16))
            scores = [_dot_nt(pr['lhs'][i], pr['kss'][i]) for i in range(n_sub)]
            upd = _dot(pr['vt'], pr['kd'])
            state[2 * mi + p] = (sT * pr['decay'] + upd) * bd
            outs.append((o_inter, scores))

        for idx, (mi, p) in enumerate(pairs):
            o_inter, scores = outs[idx]
            vb = prep[idx]['vb']
            parts = []
            for i in range(n_sub):
                lo, n = i * GLA_SUB, (i + 1) * GLA_SUB
                tt = lo + (lax.broadcasted_iota(I32, (2 * GLA_SUB, n), 0) % GLA_SUB)
                ss = lax.broadcasted_iota(I32, (2 * GLA_SUB, n), 1)
                a = jnp.where(ss <= tt, scores[i], 0.0).astype(BF16)
                oi = _dot(a, vb[0:n, :])
                parts.append(jnp.where(lane16 < HEAD_DIM, oi[0:GLA_SUB, :], oi[GLA_SUB:, :]))
            o = o_inter + jnp.concatenate(parts, axis=0)
            oo = o * o
            s_lo = jnp.sum(jnp.where(lane < HEAD_DIM, oo, 0.0), axis=1, keepdims=True)
            s_hi = jnp.sum(jnp.where(lane < HEAD_DIM, 0.0, oo), axis=1, keepdims=True)
            ms = jnp.where(lane < HEAD_DIM, s_lo, s_hi) * (1.0 / HEAD_DIM)
            on = o * lax.rsqrt(ms + LN_EPS) * mix[mi][5] * mix[mi][4][:, sl(p)]
            o_ref[rows, mi * 256 + p * LANES:mi * 256 + (p + 1) * LANES] = on
        return carry

    lax.fori_loop(0, n_chunks, chunk, 0)

    @pl.when(c_idx == pl.num_programs(1) - 1)
    def _():
        sT_ref[0] = state[...]


def _gla_hgrn(g_all, h_all, misc, wg_pad, bg, lb, gn, hn, s0_bd, batch, t_pad, t_valid):
    ct = min(2 * GLA_CHUNK, t_pad)
    steps = t_pad // ct
    n = batch * t_pad
    row = lambda w: pl.BlockSpec((ct, w), lambda b, c: (b * steps + c, 0))
    const = lambda shp: pl.BlockSpec(shp, lambda b, c: tuple(0 for _ in shp))
    st = pl.BlockSpec((1, 4, LANES, LANES), lambda b, c: (b, 0, 0, 0))
    kern = functools.partial(_gla_kernel, n_chunks=ct // GLA_CHUNK,
                             t_valid=None if t_valid == t_pad else t_valid)
    return pl.pallas_call(
        kern, grid=(batch, steps),
        in_specs=[row(1024), row(1024), row(LANES), const((LANES, 256)), const((1, 256)), const((1, 256)),
                  const((1, LANES)), const((1, LANES)), st],
        out_specs=[row(512), st],
        out_shape=[jax.ShapeDtypeStruct((n, 512), F32),
                   jax.ShapeDtypeStruct((batch, 4, LANES, LANES), F32)],
        scratch_shapes=[pltpu.VMEM((4, LANES, LANES), F32)],
        compiler_params=_cparams(("parallel", "arbitrary")), name="gla_hgrn")(
            g_all, h_all, misc, wg_pad, bg, lb, gn, hn, s0_bd)


def _sort_key(x):
    bits = lax.bitcast_convert_type(x, I32)
    return jnp.where(bits < 0, bits ^ jnp.int32(0x7FFFFFFF), bits)


def _kth_largest_key(count_ge, shape, k):
    int_min = jnp.int32(-2 ** 31)

    def body(it, cand):
        trial = cand | jnp.left_shift(jnp.int32(1), 31 - it)
        cnt = count_ge(trial ^ int_min)
        return jnp.where(cnt >= k, trial, cand)

    cand = lax.fori_loop(0, 32, body, jnp.zeros(shape, I32))
    return cand ^ int_min


def _topk_additive_mask(key_ref, neg_ref, w, topk, row_pos):
    qb = key_ref.shape[0]
    count = lambda mask: jnp.sum(jnp.where(mask, 1.0, 0.0), axis=1, keepdims=True)
    thr = _kth_largest_key(lambda t: count(key_ref[:, 0:w] >= t), (qb, 1), float(topk))
    col_pos = lax.broadcasted_iota(I32, (1, w), 1)
    keys = key_ref[:, 0:w]
    neg_ref[:, 0:w] = jnp.where((keys >= thr) & (col_pos <= row_pos), 0.0, NEG_BIG)

    need_eq = float(topk) - count(keys > thr)
    n_eq = count(keys == thr)
    above_masked = thr > _sort_key(jnp.full((1, 1), -jnp.inf, F32))
    excess = jnp.max(jnp.where(above_masked, n_eq - need_eq, 0.0))

    @pl.when(excess > 0.5)
    def _():
        r = lax.broadcasted_iota(I32, (LANES, LANES), 0)
        cc = lax.broadcasted_iota(I32, (LANES, LANES), 1)
        upper = jnp.where(r <= cc, 1.0, 0.0).astype(BF16)
        seen = jnp.zeros((qb, 1), F32)
        for c in range(w // LANES):
            cols = slice(c * LANES, (c + 1) * LANES)
            kt_ = key_ref[:, cols]
            eq = jnp.where(kt_ == thr, 1.0, 0.0)
            prefix = _dot(eq.astype(BF16), upper) + seen
            keep = (kt_ > thr) | ((kt_ == thr) & (prefix <= need_eq))
            cpos = c * LANES + lax.broadcasted_iota(I32, (1, LANES), 1)
            neg_ref[:, cols] = jnp.where(keep & (cpos <= row_pos), 0.0, NEG_BIG)
            seen = prefix[:, LANES - 1:LANES]


def _dsa_prompt_kernel(q_ref, qi_ref, misc_ref, kb_ref, vb_ref, kid_ref, o_ref,
                       sc_ref, key_ref, neg_ref, *, seq, kt, topk):
    qb = q_ref.shape[0]
    i = pl.program_id(1)
    q_lo = i * qb
    n_t = seq // kt
    n_act = (q_lo + qb - 1) // kt + 1
    row_pos = q_lo + lax.broadcasted_iota(I32, (qb, 1), 0)
    lane_q = lax.broadcasted_iota(I32, (qb, LANES), 1)
    half = (lane_q < HEAD_DIM, lane_q >= HEAD_DIM)
    w8 = misc_ref[...][:, MISC_IW:MISC_IW + IDX_HEADS] * (IDX_HEADS ** -0.5)

    sc_ref[...] = jnp.full(sc_ref.shape, -jnp.inf, F32)
    for t in range(n_t):
        @pl.when(t < n_act)
        def _(t=t):
            cols = slice(t * kt, (t + 1) * kt)
            kid = kid_ref[cols, :]
            acc = jnp.zeros((qb, kt), F32)
            for j in range(IDX_HEADS // 2):
                slab = qi_ref[:, j * LANES:(j + 1) * LANES]
                for c in range(2):
                    lhs = jnp.where(half[c], slab, jnp.zeros_like(slab))
                    logit = _dot_nt(lhs, kid)
                    acc = acc + jnp.maximum(logit, 0.0) * w8[:, 2 * j + c:2 * j + c + 1]
            col_pos = t * kt + lax.broadcasted_iota(I32, (1, kt), 1)
            sc_ref[:, cols] = jnp.where(col_pos <= row_pos, acc, -jnp.inf)
    key_ref[...] = _sort_key(sc_ref[...] + 0.0)

    for nt in range(1, n_t + 1):
        @pl.when(n_act == nt)
        def _(nt=nt):
            w = nt * kt
            _topk_additive_mask(key_ref, neg_ref, w, topk, row_pos)
            for p in range(2):
                kp = kb_ref[0:w, p * LANES:(p + 1) * LANES]
                vp = vb_ref[0:w, p * LANES:(p + 1) * LANES]
                for g in range(2):
                    s = 2 * p + g
                    qs = q_ref[:, s * LANES:(s + 1) * LANES]
                    outs = []
                    for c in range(2):
                        lhs = jnp.where(half[c], qs, jnp.zeros_like(qs))
                        sc = _dot_nt(lhs, kp) + neg_ref[:, 0:w]
                        m = jnp.max(sc, axis=1, keepdims=True)
                        pe = jnp.exp(sc - m)
                        l = jnp.sum(pe, axis=1, keepdims=True)
                        outs.append(_dot(pe.astype(BF16), vp) / l)
                    o_ref[:, s * LANES:(s + 1) * LANES] = jnp.where(half[0], outs[0], outs[1])


def _dsa_prompt(q, qi, misc, kb, vb, kid, batch, seq, qb=128):
    n = batch * seq
    nq = seq // qb
    kt = min(512, seq)
    topk = min(TOPK_MAX, seq // 4)
    rowq = lambda w: pl.BlockSpec((qb, w), lambda b, i: (b * nq + i, 0))
    full = lambda w: pl.BlockSpec((seq, w), lambda b, i: (b, 0))
    kern = functools.partial(_dsa_prompt_kernel, seq=seq, kt=kt, topk=topk)
    return pl.pallas_call(
        kern, grid=(batch, nq),
        in_specs=[rowq(512), rowq(512), rowq(LANES), full(256), full(256), full(LANES)],
        out_specs=rowq(512),
        out_shape=jax.ShapeDtypeStruct((n, 512), F32),
        scratch_shapes=[pltpu.VMEM((qb, seq), F32), pltpu.VMEM((qb, seq), I32), pltpu.VMEM((qb, seq), F32)],
        compiler_params=_cparams(("parallel", "arbitrary")), name="dsa_prompt")(q, qi, misc, kb, vb, kid)


PAGES_PER_STEP = 16
DSA_Q_HEADS = 8
DSA_KV_HEADS = 4


def _dsa_sample_score_kernel(pt_ref, qi_ref, w_ref, *refs):
    pages, out_ref = refs[:PAGES_PER_STEP], refs[PAGES_PER_STEP]
    qi = qi_ref[0]
    w = w_ref[0] * (IDX_HEADS ** -0.5)
    for j, pg in enumerate(pages):
        logit = _dot(qi, pg[...].astype(BF16))
        out_ref[0, j:j + 1, :] = jnp.sum(jnp.maximum(logit, 0.0) * w, axis=0, keepdims=True)


def _dsa_sample_scores(page_table, cache_kidx_t, li, qi3, w3):
    bd, n_pages = page_table.shape
    groups = n_pages // PAGES_PER_STEP

    def page_spec(j):
        return pl.BlockSpec((None, None, HEAD_DIM, PAGE_SIZE),
                            lambda b, g, pt: (li, pt[b, g * PAGES_PER_STEP + j], 0, 0))

    grid_spec = pltpu.PrefetchScalarGridSpec(
        num_scalar_prefetch=1, grid=(bd, groups),
        in_specs=[pl.BlockSpec((1, IDX_HEADS, HEAD_DIM), lambda b, g, pt: (b, 0, 0)),
                  pl.BlockSpec((1, IDX_HEADS, 1), lambda b, g, pt: (b, 0, 0))]
        + [page_spec(j) for j in range(PAGES_PER_STEP)],
        out_specs=pl.BlockSpec((1, PAGES_PER_STEP, PAGE_SIZE), lambda b, g, pt: (b, g, 0)))
    return pl.pallas_call(
        _dsa_sample_score_kernel, grid_spec=grid_spec,
        out_shape=jax.ShapeDtypeStruct((bd, n_pages, PAGE_SIZE), F32),
        compiler_params=_cparams(("parallel", "arbitrary")), name="dsa_sample_scores")(
            page_table, qi3, w3, *([cache_kidx_t] * PAGES_PER_STEP))


def _dsa_sample_threshold_kernel(sc_ref, qi_ref, ki_ref, w_ref, thr_ref, self_ref, key_ref, *, topk):
    qi = qi_ref[...].astype(F32)
    ki = ki_ref[...].astype(BF16).astype(F32)
    w = w_ref[...] * (IDX_HEADS ** -0.5)
    logit = jnp.sum(qi * ki, axis=2, keepdims=True)
    self_sc = jnp.sum(jnp.maximum(logit, 0.0) * w, axis=1)
    self_key = _sort_key(self_sc)
    key_ref[...] = _sort_key(sc_ref[...])

    def count_ge(thr):
        c = jnp.sum(jnp.where(key_ref[...] >= thr, 1.0, 0.0), axis=1, keepdims=True)
        return c + jnp.where(self_key >= thr, 1.0, 0.0)

    thr = _kth_largest_key(count_ge, self_key.shape, float(topk))
    thr_ref[...] = thr
    self_ref[...] = jnp.where(self_key >= thr, 1.0, 0.0)


def _dsa_sample_threshold(scores2, qi3, ki3, w3, topk):
    bd, past = scores2.shape
    kern = functools.partial(_dsa_sample_threshold_kernel, topk=topk)
    return pl.pallas_call(
        kern,
        out_shape=[jax.ShapeDtypeStruct((bd, 1), I32), jax.ShapeDtypeStruct((bd, 1), F32)],
        scratch_shapes=[pltpu.VMEM((bd, past), I32)],
        compiler_params=pltpu.CompilerParams(vmem_limit_bytes=VMEM_LIMIT),
        name="dsa_sample_threshold")(scores2, qi3, ki3, w3)


def _dsa_sample_attn_kernel(pt_ref, qbd_ref, sc_ref, thr_ref, self_ref, kn_ref, vn_ref, *refs,
                            n_groups):
    kpages = refs[:PAGES_PER_STEP]
    vpages = refs[PAGES_PER_STEP:2 * PAGES_PER_STEP]
    o_ref, m_ref, l_ref, acc_ref = refs[2 * PAGES_PER_STEP:]
    g = pl.program_id(1)
    nh = qbd_ref.shape[1]

    @pl.when(g == 0)
    def _():
        m_ref[...] = jnp.full(m_ref.shape, NEG_BIG, F32)
        l_ref[...] = jnp.zeros(l_ref.shape, F32)
        acc_ref[...] = jnp.zeros(acc_ref.shape, F32)

    qbd = qbd_ref[0]
    thr = thr_ref[0]

    for j in range(PAGES_PER_STEP):
        keep = _sort_key(sc_ref[0, j:j + 1, :]) >= thr
        sc = jnp.where(keep, _dot(qbd, kpages[j][...].astype(BF16)), NEG_BIG)
        m_old = m_ref[...]
        m_new = jnp.maximum(m_old, jnp.max(sc, axis=1, keepdims=True))
        alpha = jnp.exp(m_old - m_new)
        pe = jnp.where(keep, jnp.exp(sc - m_new), 0.0)
        l_ref[...] = alpha * l_ref[...] + jnp.sum(pe, axis=1, keepdims=True)
        acc_ref[...] = alpha * acc_ref[...] + _dot_nt(pe.astype(BF16), vpages[j][...].astype(BF16))
        m_ref[...] = m_new

    @pl.when(g == n_groups - 1)
    def _():
        kn = kn_ref[0].astype(BF16).astype(F32)
        sc = jnp.sum(qbd.astype(F32) * kn, axis=1, keepdims=True)
        sc = jnp.where(self_ref[0] > 0.5, sc, NEG_BIG)
        m_old = m_ref[...]
        m_new = jnp.maximum(m_old, sc)
        alpha = jnp.exp(m_old - m_new)
        pe = jnp.exp(sc - m_new)
        pe = jnp.where(sc > 0.5 * NEG_BIG, pe, 0.0)
        l = alpha * l_ref[...] + pe
        acc = alpha * acc_ref[...] + pe * vn_ref[0].astype(BF16).astype(F32)
        res = acc / l
        kvh = lax.broadcasted_iota(I32, (nh, HEAD_DIM), 0) // 2
        out = jnp.zeros((nh, HEAD_DIM), F32)
        for c in range(4):
            out = out + jnp.where(kvh == c, res[:, c * HEAD_DIM:(c + 1) * HEAD_DIM], 0.0)
        o_ref[0] = out


def _dsa_sample_attention(page_table, cache_k2, cache_v2, li, qbd, scores3, thr3, self3, kn3, vn3):
    bd, n_pages = page_table.shape
    groups = n_pages // PAGES_PER_STEP
    kvw = cache_k2.shape[-2]

    def page_spec(j):
        return pl.BlockSpec((None, None, kvw, PAGE_SIZE),
                            lambda b, g, pt: (li, pt[b, g * PAGES_PER_STEP + j], 0, 0))

    per_b = lambda shp: pl.BlockSpec((1,) + shp, lambda b, g, pt: (b, 0, 0))
    grid_spec = pltpu.PrefetchScalarGridSpec(
        num_scalar_prefetch=1, grid=(bd, groups),
        in_specs=[per_b((DSA_Q_HEADS, kvw)),
                  pl.BlockSpec((1, PAGES_PER_STEP, PAGE_SIZE), lambda b, g, pt: (b, g, 0)),
                  per_b((1, 1)), per_b((1, 1)), per_b((1, kvw)), per_b((1, kvw))]
        + [page_spec(j) for j in range(PAGES_PER_STEP)] * 2,
        out_specs=per_b((DSA_Q_HEADS, HEAD_DIM)),
        scratch_shapes=[pltpu.VMEM((DSA_Q_HEADS, 1), F32), pltpu.VMEM((DSA_Q_HEADS, 1), F32),
                        pltpu.VMEM((DSA_Q_HEADS, kvw), F32)])
    kern = functools.partial(_dsa_sample_attn_kernel, n_groups=groups)
    return pl.pallas_call(
        kern, grid_spec=grid_spec,
        out_shape=jax.ShapeDtypeStruct((bd, DSA_Q_HEADS, HEAD_DIM), F32),
        compiler_params=_cparams(("parallel", "arbitrary")), name="dsa_sample_attention")(
            page_table, qbd, scores3, thr3, self3, kn3, vn3,
            *([cache_k2] * PAGES_PER_STEP), *([cache_v2] * PAGES_PER_STEP))


def _post_mix_kernel(oa_ref, od_ref, x_ref, p_ref, woa_ref, wod_ref, g1_ref, b1_ref, wr_ref, br_ref,
                     wpg_ref, wpp_ref, cnt0_ref, x1t_ref, base_ref, route_ref, cnt_ref):
    tb = x_ref.shape[0]

    @pl.when(pl.program_id(0) == 0)
    def _():
        cnt_ref[...] = cnt0_ref[...]

    mixv = _dot(oa_ref[...].astype(BF16), woa_ref[...]) + _dot(od_ref[...].astype(BF16), wod_ref[...])
    x1 = _layer_norm_rows(DEEPNORM_ALPHA * x_ref[...] + mixv, g1_ref[...], b1_ref[...])
    x1b = x1.astype(BF16)
    for j in range(SUBLANES):
        x1t_ref[pl.ds(j, tb, stride=SUBLANES), :] = x1[:, j * LANES:(j + 1) * LANES]
    ple = _sigmoid(_dot(x1b, wpg_ref[...])) * _dot(p_ref[...].astype(BF16), wpp_ref[...])
    base_ref[...] = DEEPNORM_ALPHA * x1 + ple

    logits = _dot(x1, wr_ref[...], precision=lax.Precision.HIGHEST) + br_ref[...]
    lane = lax.broadcasted_iota(I32, (tb, LANES), 1)
    lanef = lane.astype(F32)
    vals, hots = [], []
    work = logits
    for _ in range(TOP_K):
        m = jnp.max(work, axis=1, keepdims=True)
        idx = jnp.min(jnp.where(work == m, lanef, float(LANES)), axis=1, keepdims=True)
        hot = lanef == idx
        vals.append(m)
        hots.append(hot)
        work = jnp.where(hot, -jnp.inf, work)
    exps = [jnp.exp(v - vals[0]) for v in vals]
    den = exps[0] + exps[1] + exps[2] + exps[3]
    oh = jnp.zeros((tb, LANES), F32)
    for hot in hots:
        oh = oh + jnp.where(hot, 1.0, 0.0)
    r = lax.broadcasted_iota(I32, (tb, tb), 0)
    c = lax.broadcasted_iota(I32, (tb, tb), 1)
    strict = jnp.where(c < r, 1.0, 0.0).astype(BF16)
    before = _dot(strict, oh.astype(BF16)) + cnt_ref[...]
    route = jnp.zeros((tb, LANES), F32)
    for k in range(TOP_K):
        gate = exps[k] / den
        eidx = jnp.sum(jnp.where(hots[k], lanef, 0.0), axis=1, keepdims=True)
        rank = jnp.sum(jnp.where(hots[k], before, 0.0), axis=1, keepdims=True)
        route = route + jnp.where(lane == k, gate, 0.0) + jnp.where(lane == TOP_K + k, eidx, 0.0) \
            + jnp.where(lane == 2 * TOP_K + k, rank, 0.0)
    route_ref[...] = route
    cnt_ref[...] = cnt_ref[...] + jnp.sum(oh, axis=0, keepdims=True)


def _post_mix(oa, od, x, p, woa, wod, g1, b1, wr, br, wpg, wpp, cnt0, tb):
    n = x.shape[0]
    row = lambda w: pl.BlockSpec((tb, w), lambda i: (i, 0))
    const = lambda a: pl.BlockSpec(a.shape, lambda i: tuple(0 for _ in a.shape))
    return pl.pallas_call(
        _post_mix_kernel, grid=(n // tb,),
        in_specs=[row(512), row(512), row(D_MODEL), row(p.shape[1]), const(woa), const(wod), const(g1),
                  const(b1), const(wr), const(br), const(wpg), const(wpp), const(cnt0)],
        out_specs=[pl.BlockSpec((tb * SUBLANES, LANES), lambda i: (i, 0)), row(D_MODEL), row(LANES),
                   pl.BlockSpec((1, LANES), lambda i: (0, 0))],
        out_shape=[jax.ShapeDtypeStruct((n * SUBLANES, LANES), F32), jax.ShapeDtypeStruct((n, D_MODEL), F32),
                   jax.ShapeDtypeStruct((n, LANES), F32), jax.ShapeDtypeStruct((1, LANES), F32)],
        compiler_params=_cparams(("arbitrary",)), name="post_mix_router")(
            oa, od, x, p, woa, wod, g1, b1, wr, br, wpg, wpp, cnt0)


def _slot_table_kernel(dest_ref, tab_ref):
    unroll = 16

    def clear(c, carry):
        for u in range(unroll):
            tab_ref[c * unroll + u] = jnp.int32(-1)
        return carry

    lax.fori_loop(0, tab_ref.shape[0] // unroll, clear, 0)

    def place(c, carry):
        for u in range(unroll):
            a = c * unroll + u
            tab_ref[dest_ref[a]] = a
        return carry

    lax.fori_loop(0, dest_ref.shape[0] // unroll, place, 0)


def _slot_table(dest, n_blocks):
    return pl.pallas_call(
        _slot_table_kernel,
        in_specs=[pl.BlockSpec(memory_space=pltpu.SMEM)],
        out_specs=pl.BlockSpec(memory_space=pltpu.SMEM),
        out_shape=jax.ShapeDtypeStruct((n_blocks * MOE_BLK,), I32),
        name="moe_slot_table")(dest.reshape(-1))


def _fused_row_copy(src, src_row, dst, dst_row, sem):
    return pltpu.make_async_copy(src.at[pl.ds(src_row * SUBLANES, SUBLANES), :],
                                 dst.at[pl.ds(dst_row * SUBLANES, SUBLANES), :], sem)


def _moe_fused_kernel(be_ref, br_ref, nu_ref, tab_cur, tab_nxt, x_hbm, wg_ref, bg_ref, wu_ref, bu_ref, wd_ref,
                      bd_ref, y_hbm, xbuf, ybuf, gsem, ssem, wgb, wub, wdb, *, n_tok):
    i = pl.program_id(0)
    n_used = nu_ref[0]
    slot = i % 2

    def for_real_slots(n_real, fn):
        def group(c, carry):
            for u in range(DMA_UNROLL):
                fn(c * DMA_UNROLL + u)
            return carry

        n_full = lax.shift_right_logical(n_real, DMA_UNROLL_LOG2)
        lax.fori_loop(0, n_full, group, 0)

        def single(s, carry):
            fn(s)
            return carry

        lax.fori_loop(n_full * DMA_UNROLL, n_real, single, 0)

    def wait_rows(sem, n_real):
        for_real_slots(n_real, lambda s: _fused_row_copy(x_hbm, 0, xbuf.at[0], 0, sem).wait())

    def gather(tab, dst_slot, n_real):
        def one(s):
            tok = lax.shift_right_logical(tab[0, 0, s], TOP_K_LOG2)
            _fused_row_copy(x_hbm, tok, xbuf.at[dst_slot], s, gsem.at[dst_slot]).start()

        for_real_slots(n_real, one)

    def scatter(tab, src_slot, n_real):
        def one(s):
            a = tab[0, 0, s]
            row = (a & (TOP_K - 1)) * n_tok + lax.shift_right_logical(a, TOP_K_LOG2)
            _fused_row_copy(ybuf.at[src_slot], s, y_hbm, row, ssem.at[src_slot]).start()

        for_real_slots(n_real, one)

    real = lambda blk: br_ref[jnp.clip(blk, 0, n_used - 1)]

    @pl.when(i < n_used)
    def _():
        @pl.when(i == 0)
        def _():
            xbuf[...] = jnp.zeros(xbuf.shape, F32)
            gather(tab_cur, 0, real(0))

        @pl.when(i + 1 < n_used)
        def _():
            gather(tab_nxt, 1 - slot, real(i + 1))

        wait_rows(gsem.at[slot], real(i))

        @pl.when(i >= 2)
        def _():
            wait_rows(ssem.at[slot], real(i - 2))

        prev = be_ref[jnp.maximum(i - 1, 0)]

        @pl.when((i == 0) | (prev != be_ref[i]))
        def _():
            wgb[...] = wg_ref[...].astype(BF16)
            wub[...] = wu_ref[...].astype(BF16)
            wdb[...] = wd_ref[...].astype(BF16)

        x = jnp.concatenate([xbuf[slot, pl.ds(j, MOE_BLK, stride=SUBLANES), :] for j in range(SUBLANES)],
                            axis=1).astype(BF16)
        g = jnp.minimum(_dot(x, wgb[...]) + bg_ref[...], SWIGLU_LIMIT)
        u = jnp.clip(_dot(x, wub[...]) + bu_ref[...], -SWIGLU_LIMIT, SWIGLU_LIMIT)
        hdn = (u + 1.0) * g * _sigmoid(SWIGLU_ALPHA * g)
        y = _dot(hdn.astype(BF16), wdb[...]) + bd_ref[...]
        for j in range(SUBLANES):
            ybuf[slot, pl.ds(j, MOE_BLK, stride=SUBLANES), :] = y[:, j * LANES:(j + 1) * LANES]
        scatter(tab_cur, slot, real(i))

        @pl.when(i == n_used - 1)
        def _():
            wait_rows(ssem.at[slot], real(i))

            @pl.when(i >= 1)
            def _():
                wait_rows(ssem.at[1 - slot], real(i - 1))


def _moe_fused(block_expert, block_real, n_used, tab3, x1t, li, wg, bg, wu, bu, wd, bd):
    n_blocks = block_expert.shape[0]
    n_tok = x1t.shape[0] // SUBLANES
    d, f = wg.shape[2], wg.shape[3]

    def w_idx(i, be, br, nu):
        return (li, be[jnp.minimum(i, nu[0] - 1)], 0, 0)

    tab_spec = lambda step: pl.BlockSpec(
        (1, 1, MOE_BLK), lambda i, be, br, nu: (jnp.minimum(i + step, nu[0] - 1), 0, 0),
        memory_space=pltpu.SMEM)
    grid_spec = pltpu.PrefetchScalarGridSpec(
        num_scalar_prefetch=3, grid=(n_blocks,),
        in_specs=[tab_spec(0), tab_spec(1), pl.BlockSpec(memory_space=pl.ANY),
                  pl.BlockSpec((None, None, d, f), w_idx), pl.BlockSpec((None, None, 1, f), w_idx),
                  pl.BlockSpec((None, None, d, f), w_idx), pl.BlockSpec((None, None, 1, f), w_idx),
                  pl.BlockSpec((None, None, f, d), w_idx), pl.BlockSpec((None, None, 1, d), w_idx)],
        out_specs=pl.BlockSpec(memory_space=pl.ANY),
        scratch_shapes=[pltpu.VMEM((2, MOE_BLK * SUBLANES, LANES), F32),
                        pltpu.VMEM((2, MOE_BLK * SUBLANES, LANES), F32),
                        pltpu.SemaphoreType.DMA((2,)), pltpu.SemaphoreType.DMA((2,)),
                        pltpu.VMEM((d, f), BF16), pltpu.VMEM((d, f), BF16), pltpu.VMEM((f, d), BF16)])
    kern = functools.partial(_moe_fused_kernel, n_tok=n_tok)
    return pl.pallas_call(
        kern, grid_spec=grid_spec,
        out_shape=jax.ShapeDtypeStruct((TOP_K * n_tok * SUBLANES, LANES), F32),
        compiler_params=_cparams(("arbitrary",)), name="moe_experts")(
            block_expert, block_real, n_used, tab3, tab3, x1t, wg, bg, wu, bu, wd, bd)


def _moe_sum_kernel(y0_ref, y1_ref, y2_ref, y3_ref, route_ref, base_ref, g2_ref, b2_ref, o_ref):
    tb = base_ref.shape[0]
    gates = route_ref[...][:, 0:TOP_K]
    cols = []
    for j in range(SUBLANES):
        acc = jnp.zeros((tb, LANES), F32)
        for k, y_ref in enumerate((y0_ref, y1_ref, y2_ref, y3_ref)):
            acc = acc + gates[:, k:k + 1] * y_ref[pl.ds(j, tb, stride=SUBLANES), :]
        cols.append(acc)
    ffn = jnp.concatenate(cols, axis=1)
    o_ref[...] = _layer_norm_rows(base_ref[...] + ffn, g2_ref[...], b2_ref[...])


def _moe_sum(y_tok, route, base, g2, b2, tb):
    n = base.shape[0]
    steps = n // tb
    row = lambda w: pl.BlockSpec((tb, w), lambda i: (i, 0))
    const = lambda a: pl.BlockSpec(a.shape, lambda i: tuple(0 for _ in a.shape))
    y_spec = lambda k: pl.BlockSpec((tb * SUBLANES, LANES), lambda i: (k * steps + i, 0))
    return pl.pallas_call(
        _moe_sum_kernel, grid=(steps,),
        in_specs=[y_spec(k) for k in range(TOP_K)] + [row(LANES), row(D_MODEL), const(g2), const(b2)],
        out_specs=row(D_MODEL),
        out_shape=jax.ShapeDtypeStruct((n, D_MODEL), F32),
        compiler_params=_cparams(("parallel",)), name="moe_combine")(
            *([y_tok] * TOP_K), route, base, g2, b2)


def _moe_layer(x1t, route, cnt, base, lw, ew, li, tb):
    n = base.shape[0]
    a = n * TOP_K
    n_blocks = a // MOE_BLK + N_EXPERTS
    counts = cnt[0, :N_EXPERTS].astype(I32)
    padded = (counts + MOE_BLK - 1) // MOE_BLK * MOE_BLK
    pad_end = jnp.cumsum(padded)
    pad_start = pad_end - padded
    top_idx = route[:, TOP_K:2 * TOP_K].astype(I32)
    rank = route[:, 2 * TOP_K:3 * TOP_K].astype(I32)
    dest = pad_start[top_idx] + rank
    block_start = jnp.arange(n_blocks, dtype=I32) * MOE_BLK
    block_expert = jnp.minimum(jnp.sum((pad_end[None, :] <= block_start[:, None]).astype(I32), axis=1),
                               N_EXPERTS - 1).astype(I32)
    n_used = (pad_end[-1:] // MOE_BLK).astype(I32)
    tab3 = _slot_table(dest, n_blocks).reshape(n_blocks, 1, MOE_BLK)
    block_real = jnp.clip(counts[block_expert] - (block_start - pad_start[block_expert]), 0, MOE_BLK)
    y_tok = _moe_fused(block_expert, block_real.astype(I32), n_used, tab3, x1t, li, *ew)
    return _moe_sum(y_tok, route, base, lw['ln2_g'], lw['ln2_b'], tb)


_DQ_PROMPT_ORDER = (0, 2, 1, 3, 4, 6, 5, 7)


def _relayout_w_in(w, head_order):
    col = lambda name, width: w[:, _OFF[name]:_OFF[name] + width]
    dq = col('dq', 512).reshape(D_MODEL, 8, HEAD_DIM)[:, jnp.array(head_order), :].reshape(D_MODEL, 512)
    z = lambda width: jnp.zeros((D_MODEL, width), w.dtype)
    misc = jnp.concatenate([col('ik', 64), z(16), col('iw', 8), col('glr', 16), z(24)], axis=1)
    out = jnp.concatenate([col('gq', 256), col('gk', 256), col('gv', 256), col('gr', 256),
                           col('hq', 256), col('hf', 256), col('hi', 256), col('hgt', 256),
                           dq, col('dk', 256), col('dv', 256), col('iq', 512), misc], axis=1)
    return out.astype(BF16)


def _rope_tables(pos):
    half = ROPE_DIM // 2
    inv = ROPE_THETA ** (-jnp.arange(half, dtype=F32) / half)
    ang = pos.astype(F32)[:, None] * inv[None, :]
    cos, sin = jnp.cos(ang), jnp.sin(ang)
    t = pos.shape[0]
    one, zero = jnp.ones((t, HEAD_DIM - ROPE_DIM), F32), jnp.zeros((t, HEAD_DIM - ROPE_DIM), F32)
    z8 = jnp.zeros((t, half), F32)
    c64 = jnp.concatenate([cos, cos, one], axis=1)
    a64 = jnp.concatenate([-sin, z8, zero], axis=1)
    b64 = jnp.concatenate([z8, sin, zero], axis=1)
    tile2 = lambda a: jnp.concatenate([a, a], axis=1)
    return tile2(c64), tile2(a64), tile2(b64)


def _layer_weights(li, w):
    wo = w['w_out'][li]
    wod = wo[512:].reshape(8, HEAD_DIM, D_MODEL)
    wg_pad = jnp.zeros((LANES, 256), F32).at[MISC_GLR:MISC_GLR + 16].set(w['w_gla_gate'][li])
    wr = jnp.zeros((D_MODEL, LANES), F32).at[:, :N_EXPERTS].set(w['w_router'][li])
    br = jnp.full((1, LANES), NEG_BIG, F32).at[0, :N_EXPERTS].set(w['b_router'][li])
    tile2 = lambda g: jnp.concatenate([g, g]).reshape(1, LANES)
    return dict(
        w_in_p=_relayout_w_in(w['w_in'][li], _DQ_PROMPT_ORDER),
        wg_pad=wg_pad, bg=w['b_gla_gate'][li].reshape(1, 256),
        gn=tile2(w['gla_norm_g'][li]), hn=tile2(w['hg_norm_g'][li]),
        woa=wo[:512].astype(BF16),
        wod_p=wod[jnp.array(_DQ_PROMPT_ORDER)].reshape(512, D_MODEL).astype(BF16),
        ln1_g=w['ln1_g'][li].reshape(1, D_MODEL), ln1_b=w['ln1_b'][li].reshape(1, D_MODEL),
        wr=wr, br=br,
        wpg=w['w_ple_gate'][li].astype(BF16), wpp=w['w_ple_proj'][li].astype(BF16),
        ln2_g=w['ln2_g'][li].reshape(1, D_MODEL), ln2_b=w['ln2_b'][li].reshape(1, D_MODEL))


def _state_to_blockdiag(s):
    b = s.shape[0]
    st = jnp.swapaxes(s, -1, -2).reshape(b, 2, 2, HEAD_DIM, HEAD_DIM)
    z = jnp.zeros_like(st[:, :, 0])
    top = jnp.concatenate([st[:, :, 0], z], axis=-1)
    bot = jnp.concatenate([z, st[:, :, 1]], axis=-1)
    return jnp.concatenate([top, bot], axis=-2)


def _state_from_blockdiag(sbd):
    b = sbd.shape[0]
    h0 = sbd[:, :, :HEAD_DIM, :HEAD_DIM]
    h1 = sbd[:, :, HEAD_DIM:, HEAD_DIM:]
    st = jnp.stack([h0, h1], axis=2).reshape(b, 4, HEAD_DIM, HEAD_DIM)
    return jnp.swapaxes(st, -1, -2)


def kernel(x_prompt, x_sample, cache_k, cache_v, cache_kidx, state_gla, state_hgrn, page_table, p_prompt, p_sample, ln_in_g, ln_in_b, w_in, w_gla_gate, b_gla_gate, gla_norm_g, hg_gamma, hg_norm_g, w_out, ln1_g, ln1_b, w_router, b_router, w_exp_gate, b_exp_gate, w_exp_up, b_exp_up, w_exp_down, b_exp_down, ln2_g, ln2_b, w_ple_gate, w_ple_proj):
    bp, t, d = x_prompt.shape
    bs, ts, _ = x_sample.shape
    assert ts == 1 and d == D_MODEL
    depth = w_in.shape[0]
    n_pages = page_table.shape[1]
    past = n_pages * PAGE_SIZE
    n_p = bp * t
    weights = dict(w_in=w_in, w_gla_gate=w_gla_gate, b_gla_gate=b_gla_gate, gla_norm_g=gla_norm_g,
                   hg_norm_g=hg_norm_g, w_out=w_out, ln1_g=ln1_g, ln1_b=ln1_b, w_router=w_router,
                   b_router=b_router, w_exp_gate=w_exp_gate, b_exp_gate=b_exp_gate, w_exp_up=w_exp_up,
                   b_exp_up=b_exp_up, w_exp_down=w_exp_down, b_exp_down=b_exp_down, ln2_g=ln2_g,
                   ln2_b=ln2_b, w_ple_gate=w_ple_gate, w_ple_proj=w_ple_proj)

    tm_p = min(512, t)
    tb_p = min(256, t)
    lbs = _lower_bounds(hg_gamma)
    tabs_p = _rope_tables(jnp.arange(t, dtype=I32))
    tabs_s = _rope_tables(jnp.full((bs,), past, I32))
    xp = _input_layer_norm(x_prompt.reshape(n_p, d), ln_in_g, ln_in_b, tm_p)
    xs = _input_layer_norm(x_sample.reshape(bs, d), ln_in_g, ln_in_b, bs)
    pages_t = lambda c: jnp.transpose(c, (0, 1, 3, 4, 2)).reshape(c.shape[0], c.shape[1], -1, PAGE_SIZE)
    cache_k2, cache_v2 = pages_t(cache_k), pages_t(cache_v)
    cache_kidx_t = jnp.transpose(cache_kidx, (0, 1, 3, 2))
    bias4 = lambda b: b.reshape(b.shape[0], b.shape[1], 1, b.shape[2])
    ew = (w_exp_gate, bias4(b_exp_gate), w_exp_up, bias4(b_exp_up), w_exp_down, bias4(b_exp_down))
    head_perm = jnp.array(_DQ_PROMPT_ORDER)
    zero_state = jnp.zeros((bp, 4, LANES, LANES), F32)
    zero_cnt = jnp.zeros((1, LANES), F32)
    topk_s = min(TOPK_MAX, (past + ts) // 4)
    head_of_row = jnp.arange(DSA_Q_HEADS)[:, None] // 2
    kv_of_lane = jnp.arange(DSA_KV_HEADS * HEAD_DIM)[None, :] // HEAD_DIM

    outs = {k: [] for k in ('kp', 'vp', 'kip', 'sgp', 'shp', 'ks', 'vs', 'kis', 'sgs', 'shs')}
    for li in range(depth):
        lw = _layer_weights(li, weights)
        lb = lbs[li].reshape(1, 256)

        (g_all, h_all, misc, q, qi, k, v, ki, kb, vb, kid) = _in_projection(xp, lw['w_in_p'], *tabs_p, tm_p)
        o_lin, st = _gla_hgrn(g_all, h_all, misc, lw['wg_pad'], lw['bg'], lb, lw['gn'], lw['hn'],
                              zero_state, bp, t, t)
        o_dsa = _dsa_prompt(q, qi, misc, kb, vb, kid, bp, t)
        x1t, base, route, cnt = _post_mix(o_lin, o_dsa, xp, p_prompt[li].reshape(n_p, -1), lw['woa'],
                                          lw['wod_p'], lw['ln1_g'], lw['ln1_b'], lw['wr'], lw['br'],
                                          lw['wpg'], lw['wpp'], zero_cnt, tb_p)
        xp = _moe_layer(x1t, route, cnt, base, lw, ew, li, tb_p)
        outs['kp'].append(k.reshape(bp, t, DSA_KV_HEADS, HEAD_DIM))
        outs['vp'].append(v.reshape(bp, t, DSA_KV_HEADS, HEAD_DIM))
        outs['kip'].append(ki.reshape(bp, t, HEAD_DIM))
        outs['sgp'].append(_state_from_blockdiag(st[:, 0:2]))
        outs['shp'].append(_state_from_blockdiag(st[:, 2:4]))

        (g_all, h_all, misc, q, qi, k, v, ki, kb, vb, kid) = _in_projection(xs, lw['w_in_p'], *tabs_s, bs)
        pad = lambda a: jnp.pad(a.reshape(bs, 1, -1), ((0, 0), (0, GLA_CHUNK - 1), (0, 0))).reshape(
            bs * GLA_CHUNK, -1)
        s0 = jnp.concatenate([_state_to_blockdiag(state_gla[li]), _state_to_blockdiag(state_hgrn[li])], axis=1)
        o_lin, st = _gla_hgrn(pad(g_all), pad(h_all), pad(misc), lw['wg_pad'], lw['bg'], lb, lw['gn'],
                              lw['hn'], s0, bs, GLA_CHUNK, 1)
        o_lin = o_lin.reshape(bs, GLA_CHUNK, -1)[:, 0]
        qi3 = qi.reshape(bs, IDX_HEADS, HEAD_DIM)
        w3 = misc[:, MISC_IW:MISC_IW + IDX_HEADS].reshape(bs, IDX_HEADS, 1)
        scores = _dsa_sample_scores(page_table, cache_kidx_t, li, qi3, w3)
        thr, self_sel = _dsa_sample_threshold(scores.reshape(bs, past), qi3, ki.reshape(bs, 1, HEAD_DIM),
                                              w3, topk_s)
        q3 = q.reshape(bs, DSA_Q_HEADS, HEAD_DIM)[:, head_perm].reshape(bs, DSA_Q_HEADS, 1, HEAD_DIM)
        qbd = jnp.where(head_of_row == kv_of_lane,
                        jnp.tile(q3, (1, 1, DSA_KV_HEADS, 1)).reshape(bs, DSA_Q_HEADS, -1), 0).astype(BF16)
        o_dsa = _dsa_sample_attention(page_table, cache_k2, cache_v2, li, qbd, scores,
                                      thr.reshape(bs, 1, 1), self_sel.reshape(bs, 1, 1),
                                      k.reshape(bs, 1, -1), v.reshape(bs, 1, -1))
        o_dsa = o_dsa[:, head_perm].reshape(bs, 512)
        x1t, base, route, cnt = _post_mix(o_lin, o_dsa, xs, p_sample[li].reshape(bs, -1), lw['woa'],
                                          lw['wod_p'], lw['ln1_g'], lw['ln1_b'], lw['wr'], lw['br'],
                                          lw['wpg'], lw['wpp'], zero_cnt, bs)
        xs = _moe_layer(x1t, route, cnt, base, lw, ew, li, bs)
        outs['ks'].append(k.reshape(bs, ts, DSA_KV_HEADS, HEAD_DIM))
        outs['vs'].append(v.reshape(bs, ts, DSA_KV_HEADS, HEAD_DIM))
        outs['kis'].append(ki.reshape(bs, ts, HEAD_DIM))
        outs['sgs'].append(_state_from_blockdiag(st[:, 0:2]))
        outs['shs'].append(_state_from_blockdiag(st[:, 2:4]))

    stack = lambda name: jnp.stack(outs[name], 0)
    return (xp.reshape(bp, t, d), xs.reshape(bs, ts, d),
            stack('kp'), stack('vp'), stack('kip'), stack('sgp'), stack('shp'),
            stack('ks'), stack('vs'), stack('kis'), stack('sgs'), stack('shs'))
```
